```python
import math
import jax, jax.numpy as jnp
from jax import lax
import numpy as np

D_MODEL = 1024
BATCH = 8
SEQ = 2048
DEPTH = 1
DEC_BATCH = 8
DEC_SEQ = 8192
PAST_LEN = 128

D_MIX = D_MODEL
D_S5 = D_MIX // 2
S5_GROUP = 16
S5_GROUPS = D_S5 // S5_GROUP
S5_STATE = 64
D_GMLP = D_MIX - D_S5
GMLP_HEADS = 4
GMLP_HEAD_DIM = D_GMLP // GMLP_HEADS
CHUNK = 128
SCAN_CHUNK = 128
D_FF = ((8 * D_MODEL // 3 + 127) // 128) * 128
N_MOD = 9
EPS = 1e-6

kernel_name = "hymba_s5_gmlp_macaron_adaln_encoder"


def rmsnorm(x, g):
    xf = x.astype(jnp.float32)
    y = xf * lax.rsqrt(jnp.mean(xf * xf, axis=-1, keepdims=True) + EPS)
    return (y * g.astype(jnp.float32)).astype(x.dtype)


def layernorm(x, g, b):
    xf = x.astype(jnp.float32)
    mu = jnp.mean(xf, axis=-1, keepdims=True)
    xc = xf - mu
    y = xc * lax.rsqrt(jnp.mean(xc * xc, axis=-1, keepdims=True) + EPS)
    return (y * g.astype(jnp.float32) + b.astype(jnp.float32)).astype(x.dtype)


def modulate(h, shift, scale):
    return h * (1 + scale[:, None, :]) + shift[:, None, :]


def swiglu(h, w_in, w_out):
    gate, up = jnp.split(h @ w_in, 2, axis=-1)
    return (jax.nn.silu(gate) * up) @ w_out


def s5_discretise(lam_re, lam_im, log_step, b_re, b_im):
    dt = jnp.exp(log_step)[:, None]
    mag = jnp.exp(lam_re * dt)
    ab_re = mag * jnp.cos(lam_im * dt)
    ab_im = mag * jnp.sin(lam_im * dt)
    n_re = ab_re - 1.0
    n_im = ab_im
    den = lam_re * lam_re + lam_im * lam_im
    f_re = (n_re * lam_re + n_im * lam_im) / den
    f_im = (n_im * lam_re - n_re * lam_im) / den
    bb_re = f_re[..., None] * b_re - f_im[..., None] * b_im
    bb_im = f_re[..., None] * b_im + f_im[..., None] * b_re
    return ab_re, ab_im, bb_re, bb_im


def _linear_recurrence_op(left, right):
    a1r, a1i, b1r, b1i = left
    a2r, a2i, b2r, b2i = right
    ar = a2r * a1r - a2i * a1i
    ai = a2r * a1i + a2i * a1r
    br = a2r * b1r - a2i * b1i + b2r
    bi = a2r * b1i + a2i * b1r + b2i
    return ar, ai, br, bi


def s5_scan_direction(u, lam_re, lam_im, log_step, b_re, b_im, c_re, c_im):
    bsz, seq_len, n_groups, _ = u.shape
    n_state = lam_re.shape[-1]
    ab_re, ab_im, bb_re, bb_im = s5_discretise(lam_re, lam_im, log_step, b_re, b_im)
    n_seg = seq_len // SCAN_CHUNK
    u_seg = u.reshape(bsz, n_seg, SCAN_CHUNK, n_groups, S5_GROUP).transpose(1, 0, 2, 3, 4)
    a_re = jnp.broadcast_to(ab_re, (bsz, SCAN_CHUNK, n_groups, n_state))
    a_im = jnp.broadcast_to(ab_im, (bsz, SCAN_CHUNK, n_groups, n_state))

    def step(carry, u_c):
        s_re, s_im = carry
        bu_re = jnp.einsum('gph,btgh->btgp', bb_re, u_c)
        bu_im = jnp.einsum('gph,btgh->btgp', bb_im, u_c)
        cum_re, cum_im, loc_re, loc_im = lax.associative_scan(
            _linear_recurrence_op, (a_re, a_im, bu_re, bu_im), axis=1)
        x_re = loc_re + cum_re * s_re[:, None] - cum_im * s_im[:, None]
        x_im = loc_im + cum_re * s_im[:, None] + cum_im * s_re[:, None]
        y = (jnp.einsum('ghp,btgp->btgh', c_re, x_re)
             - jnp.einsum('ghp,btgp->btgh', c_im, x_im))
        return (x_re[:, -1], x_im[:, -1]), y

    init = (jnp.zeros((bsz, n_groups, n_state), jnp.float32),
            jnp.zeros((bsz, n_groups, n_state), jnp.float32))
    _, ys = lax.scan(step, init, u_seg)
    return ys.transpose(1, 0, 2, 3, 4).reshape(bsz, seq_len, n_groups, S5_GROUP)


def s5_mixer(u, lam_re_f, lam_im_f, log_step_f, b_re_f, b_im_f, c_re_f, c_im_f,
             lam_re_b, lam_im_b, log_step_b, b_re_b, b_im_b, c_re_b, c_im_b, d, w_glu):
    bsz, seq_len, _ = u.shape
    f32 = lambda a: a.astype(jnp.float32)
    uf = f32(u)
    ug = uf.reshape(bsz, seq_len, S5_GROUPS, S5_GROUP)
    y_f = s5_scan_direction(ug, f32(lam_re_f), f32(lam_im_f), f32(log_step_f),
                            f32(b_re_f), f32(b_im_f), f32(c_re_f), f32(c_im_f))
    y_b = s5_scan_direction(ug[:, ::-1], f32(lam_re_b), f32(lam_im_b), f32(log_step_b),
                            f32(b_re_b), f32(b_im_b), f32(c_re_b), f32(c_im_b))[:, ::-1]
    y = (y_f + y_b).reshape(bsz, seq_len, D_S5) + f32(d) * uf
    y = jax.nn.gelu(y).astype(u.dtype)
    return y * jax.nn.sigmoid(y @ w_glu)


def gmlp_mixer(z, ln_g, ln_b, w_sp, b_sp):
    z = jax.nn.gelu(z)
    u, v = jnp.split(z, 2, axis=-1)
    v = layernorm(v, ln_g, ln_b)
    bsz, seq_len, _ = v.shape
    vc = v.reshape(bsz, seq_len // CHUNK, CHUNK, GMLP_HEADS, GMLP_HEAD_DIM)
    mixed = jnp.einsum('hqk,bnkhc->bnqhc', w_sp, vc) + b_sp.T[None, None, :, :, None]
    return u * mixed.reshape(bsz, seq_len, D_GMLP)


def encoder_layer(x, c, w_ada, b_ada, norm_ffn1_g, ffn1_w_in, ffn1_w_out,
                  norm_mix_g, w_mix_in,
                  s5_lam_re_f, s5_lam_im_f, s5_log_step_f, s5_b_re_f, s5_b_im_f, s5_c_re_f, s5_c_im_f,
                  s5_lam_re_b, s5_lam_im_b, s5_log_step_b, s5_b_re_b, s5_b_im_b, s5_c_re_b, s5_c_im_b,
                  s5_d, s5_w_glu, gmlp_ln_g, gmlp_ln_b, gmlp_w_sp, gmlp_b_sp,
                  norm_out_s5_g, norm_out_gmlp_g, w_mix_out,
                  norm_ffn2_g, ffn2_w_in, ffn2_w_out):
    mod = jax.nn.silu(c) @ w_ada + b_ada
    sh1, sc1, g1, sh2, sc2, g2, sh3, sc3, g3 = jnp.split(mod, N_MOD, axis=-1)
    h = modulate(rmsnorm(x, norm_ffn1_g), sh1, sc1)
    x = x + 0.5 * g1[:, None, :] * swiglu(h, ffn1_w_in, ffn1_w_out)
    h = modulate(rmsnorm(x, norm_mix_g), sh2, sc2)
    z = h @ w_mix_in
    y_s5 = s5_mixer(z[..., :D_S5],
                    s5_lam_re_f, s5_lam_im_f, s5_log_step_f, s5_b_re_f, s5_b_im_f, s5_c_re_f, s5_c_im_f,
                    s5_lam_re_b, s5_lam_im_b, s5_log_step_b, s5_b_re_b, s5_b_im_b, s5_c_re_b, s5_c_im_b,
                    s5_d, s5_w_glu)
    y_gm = gmlp_mixer(z[..., D_S5:], gmlp_ln_g, gmlp_ln_b, gmlp_w_sp, gmlp_b_sp)
    y_cat = jnp.concatenate([rmsnorm(y_s5, norm_out_s5_g), rmsnorm(y_gm, norm_out_gmlp_g)], axis=-1)
    x = x + g2[:, None, :] * (y_cat @ w_mix_out)
    h = modulate(rmsnorm(x, norm_ffn2_g), sh3, sc3)
    x = x + 0.5 * g3[:, None, :] * swiglu(h, ffn2_w_in, ffn2_w_out)
    return x


def run_trunk(x, c, layer_params, final_norm_g):
    for l in range(DEPTH):
        x = encoder_layer(x, c, *[p[l] for p in layer_params])
    return rmsnorm(x, final_norm_g)


def setup_inputs(seed: int = 0) -> dict:
    key = jax.random.key(seed)
    ks = iter(jax.random.split(key, 64))
    nrm = lambda shape, s: jax.random.normal(next(ks), shape, jnp.float32) * s
    G, P, H = S5_GROUPS, S5_STATE, S5_GROUP

    def gain(shape):
        return 1.0 + nrm(shape, 0.02)

    def s5_dir():
        lam_re = -0.5 * (1.0 + nrm((DEPTH, G, P), 0.02))
        lam_im = math.pi * jnp.arange(P, dtype=jnp.float32)[None, None, :] + nrm((DEPTH, G, P), 0.01)
        log_step = jax.random.uniform(next(ks), (DEPTH, G), jnp.float32,
                                      math.log(0.001), math.log(0.1))
        b_re = nrm((DEPTH, G, P, H), (2 * H) ** -0.5)
        b_im = nrm((DEPTH, G, P, H), (2 * H) ** -0.5)
        c_re = nrm((DEPTH, G, H, P), (2 * P) ** -0.5)
        c_im = nrm((DEPTH, G, H, P), (2 * P) ** -0.5)
        return lam_re, lam_im, log_step, b_re, b_im, c_re, c_im

    x_prompt = nrm((BATCH, SEQ, D_MODEL), 1.0)
    x_sample = nrm((DEC_BATCH, DEC_SEQ, D_MODEL), 1.0)
    c_prompt = nrm((BATCH, D_MODEL), 1.0)
    c_sample = nrm((DEC_BATCH, D_MODEL), 1.0)
    w_ada = nrm((DEPTH, D_MODEL, N_MOD * D_MODEL), 0.5 * D_MODEL ** -0.5)
    b_ada = nrm((DEPTH, N_MOD * D_MODEL), 0.01)
    norm_ffn1_g = gain((DEPTH, D_MODEL))
    ffn1_w_in = nrm((DEPTH, D_MODEL, 2 * D_FF), D_MODEL ** -0.5)
    ffn1_w_out = nrm((DEPTH, D_FF, D_MODEL), D_FF ** -0.5)
    norm_mix_g = gain((DEPTH, D_MODEL))
    w_mix_in = nrm((DEPTH, D_MODEL, D_S5 + 2 * D_GMLP), D_MODEL ** -0.5)
    lre_f, lim_f, ls_f, bre_f, bim_f, cre_f, cim_f = s5_dir()
    lre_b, lim_b, ls_b, bre_b, bim_b, cre_b, cim_b = s5_dir()
    s5_d = nrm((DEPTH, D_S5), 1.0)
    s5_w_glu = nrm((DEPTH, D_S5, D_S5), D_S5 ** -0.5)
    gmlp_ln_g = gain((DEPTH, D_GMLP))
    gmlp_ln_b = nrm((DEPTH, D_GMLP), 0.01)
    gmlp_w_sp = nrm((DEPTH, GMLP_HEADS, CHUNK, CHUNK), CHUNK ** -0.5)
    gmlp_b_sp = 1.0 + nrm((DEPTH, GMLP_HEADS, CHUNK), 0.01)
    norm_out_s5_g = gain((DEPTH, D_S5))
    norm_out_gmlp_g = gain((DEPTH, D_GMLP))
    w_mix_out = nrm((DEPTH, D_MIX, D_MODEL), D_MIX ** -0.5)
    norm_ffn2_g = gain((DEPTH, D_MODEL))
    ffn2_w_in = nrm((DEPTH, D_MODEL, 2 * D_FF), D_MODEL ** -0.5)
    ffn2_w_out = nrm((DEPTH, D_FF, D_MODEL), D_FF ** -0.5)
    final_norm_g = gain((D_MODEL,))
    return {
        "x_prompt": x_prompt, "x_sample": x_sample,
        "c_prompt": c_prompt, "c_sample": c_sample,
        "w_ada": w_ada, "b_ada": b_ada,
        "norm_ffn1_g": norm_ffn1_g, "ffn1_w_in": ffn1_w_in, "ffn1_w_out": ffn1_w_out,
        "norm_mix_g": norm_mix_g, "w_mix_in": w_mix_in,
        "s5_lam_re_f": lre_f, "s5_lam_im_f": lim_f, "s5_log_step_f": ls_f,
        "s5_b_re_f": bre_f, "s5_b_im_f": bim_f, "s5_c_re_f": cre_f, "s5_c_im_f": cim_f,
        "s5_lam_re_b": lre_b, "s5_lam_im_b": lim_b, "s5_log_step_b": ls_b,
        "s5_b_re_b": bre_b, "s5_b_im_b": bim_b, "s5_c_re_b": cre_b, "s5_c_im_b": cim_b,
        "s5_d": s5_d, "s5_w_glu": s5_w_glu,
        "gmlp_ln_g": gmlp_ln_g, "gmlp_ln_b": gmlp_ln_b,
        "gmlp_w_sp": gmlp_w_sp, "gmlp_b_sp": gmlp_b_sp,
        "norm_out_s5_g": norm_out_s5_g, "norm_out_gmlp_g": norm_out_gmlp_g,
        "w_mix_out": w_mix_out,
        "norm_ffn2_g": norm_ffn2_g, "ffn2_w_in": ffn2_w_in, "ffn2_w_out": ffn2_w_out,
        "final_norm_g": final_norm_g,
    }


def reference(x_prompt, x_sample, c_prompt, c_sample, w_ada, b_ada,
              norm_ffn1_g, ffn1_w_in, ffn1_w_out, norm_mix_g, w_mix_in,
              s5_lam_re_f, s5_lam_im_f, s5_log_step_f, s5_b_re_f, s5_b_im_f, s5_c_re_f, s5_c_im_f,
              s5_lam_re_b, s5_lam_im_b, s5_log_step_b, s5_b_re_b, s5_b_im_b, s5_c_re_b, s5_c_im_b,
              s5_d, s5_w_glu, gmlp_ln_g, gmlp_ln_b, gmlp_w_sp, gmlp_b_sp,
              norm_out_s5_g, norm_out_gmlp_g, w_mix_out,
              norm_ffn2_g, ffn2_w_in, ffn2_w_out, final_norm_g):
    layer_params = (w_ada, b_ada, norm_ffn1_g, ffn1_w_in, ffn1_w_out, norm_mix_g, w_mix_in,
                    s5_lam_re_f, s5_lam_im_f, s5_log_step_f, s5_b_re_f, s5_b_im_f, s5_c_re_f, s5_c_im_f,
                    s5_lam_re_b, s5_lam_im_b, s5_log_step_b, s5_b_re_b, s5_b_im_b, s5_c_re_b, s5_c_im_b,
                    s5_d, s5_w_glu, gmlp_ln_g, gmlp_ln_b, gmlp_w_sp, gmlp_b_sp,
                    norm_out_s5_g, norm_out_gmlp_g, w_mix_out,
                    norm_ffn2_g, ffn2_w_in, ffn2_w_out)
    y_prompt = run_trunk(x_prompt, c_prompt, layer_params, final_norm_g)
    y_sample = run_trunk(x_sample, c_sample, layer_params, final_norm_g)
    return (y_prompt, y_sample)
```

```python
import functools

import jax
import jax.numpy as jnp
from jax import lax
from jax.experimental import pallas as pl
from jax.experimental.pallas import tpu as pltpu

D_MODEL = 1024
BATCH = 8
D_S5 = 512
S5_GROUPS = 32
S5_GROUP = 16
S5_STATE = 64
N_STATE = S5_GROUPS * S5_STATE
D_GMLP = 512
GMLP_HEADS = 4
GMLP_HEAD_DIM = 128
CHUNK = 128
D_FF = 2816
N_MOD = 9
EPS = 1e-6

LANES = 128
SUBLANES = 8
PAIR_COLS = 2 * LANES
N_PAIRS = S5_GROUPS // 2
VMEM_LIMIT = 56 * 1024 * 1024

TM_FFN = 512
T_BLK = 128
ROWS_BLK = T_BLK * BATCH
SCAN_COLS = 512

f32 = jnp.float32
bf16 = jnp.bfloat16


def _const_spec(shape):
    nd = len(shape)
    return pl.BlockSpec(shape, lambda i: (0,) * nd, pipeline_mode=pl.Buffered(1))


def _rms(x, g):
    ms = jnp.mean(x * x, axis=-1, keepdims=True)
    return x * lax.rsqrt(ms + EPS) * g


def _per_batch(x, v, op):
    rows, d = x.shape
    x3 = x.reshape(rows // BATCH, BATCH, d)
    return op(x3, v[None]).reshape(rows, d)


def _modulate(h, shift, scale):
    return _per_batch(_per_batch(h, 1.0 + scale, jnp.multiply), shift, jnp.add)


def _ada_kernel(c_ref, w_ref, b_ref, o_ref):
    c = c_ref[...]
    s = c * jax.nn.sigmoid(c)
    o_ref[...] = jnp.dot(s.astype(bf16), w_ref[...].astype(bf16),
                         preferred_element_type=f32) + b_ref[...]


def _ada(c, w, b):
    n = c.shape[0]
    return pl.pallas_call(
        _ada_kernel,
        out_shape=jax.ShapeDtypeStruct((n, N_MOD * D_MODEL), f32),
        grid=(N_MOD,),
        in_specs=[pl.BlockSpec((n, D_MODEL), lambda j: (0, 0)),
                  pl.BlockSpec((D_MODEL, D_MODEL), lambda j: (0, j)),
                  pl.BlockSpec((1, D_MODEL), lambda j: (0, j))],
        out_specs=pl.BlockSpec((n, D_MODEL), lambda j: (0, j)),
        compiler_params=pltpu.CompilerParams(dimension_semantics=("arbitrary",)),
        name="ada",
    )(c, w, b)


def _ffn_kernel(x_ref, mod_ref, g_ref, win_ref, wout_ref, *rest, final):
    o_ref = rest[-1]
    x = x_ref[...]
    h = _modulate(_rms(x, g_ref[...]), mod_ref[0], mod_ref[1]).astype(bf16)
    a = jnp.dot(h, win_ref[...], preferred_element_type=f32)
    act = (jax.nn.silu(a[:, :D_FF]) * a[:, D_FF:]).astype(bf16)
    f = jnp.dot(act, wout_ref[...], preferred_element_type=f32)
    y = x + _per_batch(f, 0.5 * mod_ref[2], jnp.multiply)
    if final:
        y = _rms(y, rest[0][...])
    o_ref[...] = y


def _ffn(x, mod3, g, w_in, w_out, final_g=None):
    rows = x.shape[0]
    final = final_g is not None
    in_specs = [pl.BlockSpec((TM_FFN, D_MODEL), lambda i: (i, 0)),
                _const_spec((3, BATCH, D_MODEL)),
                _const_spec((1, D_MODEL)),
                _const_spec((D_MODEL, 2 * D_FF)),
                _const_spec((D_FF, D_MODEL))]
    args = [x, mod3, g, w_in, w_out]
    if final:
        in_specs.append(_const_spec((1, D_MODEL)))
        args.append(final_g)
    return pl.pallas_call(
        functools.partial(_ffn_kernel, final=final),
        out_shape=jax.ShapeDtypeStruct((rows, D_MODEL), f32),
        grid=(rows // TM_FFN,),
        in_specs=in_specs,
        out_specs=pl.BlockSpec((TM_FFN, D_MODEL), lambda i: (i, 0)),
        compiler_params=pltpu.CompilerParams(
            dimension_semantics=("parallel",), vmem_limit_bytes=VMEM_LIMIT),
        name="ffn_final" if final else "ffn",
    )(*args)


def _mixin_kernel(x_ref, mod_ref, g_ref, w_ref, lng_ref, lnb_ref, u_ref, ug_ref, v_ref):
    x = x_ref[...]
    h = _modulate(_rms(x, g_ref[...]), mod_ref[0], mod_ref[1]).astype(bf16)
    z = jnp.dot(h, w_ref[...], preferred_element_type=f32)
    u_ref[...] = z[:, :D_S5]
    zg = jax.nn.gelu(z[:, D_S5:])
    ug_ref[...] = zg[:, :D_GMLP]
    v = zg[:, D_GMLP:]
    mu = jnp.mean(v, axis=-1, keepdims=True)
    vc = v - mu
    var = jnp.mean(vc * vc, axis=-1, keepdims=True)
    v_ref[...] = vc * lax.rsqrt(var + EPS) * lng_ref[...] + lnb_ref[...]


def _mixin(x, mod3, g, w, ln_g, ln_b):
    rows = x.shape[0]
    half = jax.ShapeDtypeStruct((rows, D_S5), f32)
    blk = pl.BlockSpec((TM_FFN, D_S5), lambda i: (i, 0))
    return pl.pallas_call(
        _mixin_kernel,
        out_shape=(half, half, half),
        grid=(rows // TM_FFN,),
        in_specs=[pl.BlockSpec((TM_FFN, D_MODEL), lambda i: (i, 0)),
                  _const_spec((3, BATCH, D_MODEL)),
                  _const_spec((1, D_MODEL)),
                  _const_spec((D_MODEL, D_S5 + 2 * D_GMLP)),
                  _const_spec((1, D_GMLP)),
                  _const_spec((1, D_GMLP))],
        out_specs=(blk, blk, blk),
        compiler_params=pltpu.CompilerParams(
            dimension_semantics=("parallel",), vmem_limit_bytes=VMEM_LIMIT),
        name="mixin",
    )(x, mod3, g, w, ln_g, ln_b)


def _s5_prep_kernel(lre_ref, lim_ref, ls_ref, bre_ref, bim_ref,
                    are_ref, aim_ref, bbre_ref, bbim_ref):
    lre = lre_ref[...]
    lim = lim_ref[...]
    dt = jnp.exp(ls_ref[...])
    mag = jnp.exp(lre * dt)
    ab_re = mag * jnp.cos(lim * dt)
    ab_im = mag * jnp.sin(lim * dt)
    n_re = ab_re - 1.0
    n_im = ab_im
    den = lre * lre + lim * lim
    f_re = (n_re * lre + n_im * lim) / den
    f_im = (n_im * lre - n_re * lim) / den
    are_ref[...] = ab_re
    aim_ref[...] = ab_im
    bre = bre_ref[...]
    bim = bim_ref[...]
    bbre_ref[...] = f_re[:, None, :] * bre - f_im[:, None, :] * bim
    bbim_ref[...] = f_re[:, None, :] * bim + f_im[:, None, :] * bre


def _s5_prep(lam_re, lam_im, log_step, b_re, b_im):
    flat = lambda a: a.reshape(2, N_STATE)
    ls = jnp.broadcast_to(log_step[:, :, None], (2, S5_GROUPS, S5_STATE))
    bt = lambda b: b.transpose(0, 3, 1, 2).reshape(2, S5_GROUP, N_STATE)
    vec = jax.ShapeDtypeStruct((2, N_STATE), f32)
    mat = jax.ShapeDtypeStruct((2, S5_GROUP, N_STATE), f32)
    return pl.pallas_call(
        _s5_prep_kernel, out_shape=(vec, vec, mat, mat), name="s5_prep",
    )(flat(lam_re), flat(lam_im), flat(ls), bt(b_re), bt(b_im))


def _s5_b_tiles(bb_re, bb_im):
    def gph(b):
        return b.reshape(S5_GROUP, N_PAIRS, 2, S5_STATE).transpose(1, 2, 0, 3)
    both = jnp.stack([gph(bb_re), gph(bb_im)], axis=3)
    pair = jnp.einsum('ijhcp,jk->ijhckp', both, jnp.eye(2, dtype=f32))
    pair = pair.reshape(N_PAIRS, 2 * S5_GROUP, PAIR_COLS)
    pairs_per_tile = LANES // (2 * S5_GROUP)
    slot = jax.nn.one_hot(jnp.arange(N_PAIRS) % pairs_per_tile, pairs_per_tile, dtype=f32)
    w = jnp.einsum('irn,is->isrn', pair, slot)
    return w.reshape(N_PAIRS, LANES, PAIR_COLS).astype(bf16)


def _s5_c_tiles(c_re, c_im):
    half_groups = S5_GROUPS // 2
    eye = jnp.eye(half_groups, dtype=f32)

    def bd(c):
        c = c.reshape(2, half_groups, S5_GROUP, S5_STATE)
        m = jnp.einsum('aghp,gk->agpkh', c, eye)
        return m.reshape(2, half_groups * S5_STATE, half_groups * S5_GROUP)
    return jnp.concatenate([bd(c_re), -bd(c_im)], axis=1).astype(bf16)


def _s5_kernel(u_ref, wb_ref, a_ref, wc_ref, y_ref, xre_ref, xim_ref, st_ref, *, reverse):
    @pl.when(pl.program_id(0) == 0)
    def _():
        st_ref[...] = jnp.zeros_like(st_ref)

    ub = u_ref[...].astype(bf16)
    for i in range(N_PAIRS):
        lane0 = (i * 2 * S5_GROUP // LANES) * LANES
        r = jnp.dot(ub[:, lane0:lane0 + LANES], wb_ref[i], preferred_element_type=f32)
        xre_ref[:, i * LANES:(i + 1) * LANES] = r[:, :LANES]
        xim_ref[:, i * LANES:(i + 1) * LANES] = r[:, LANES:]

    for cb in range(N_STATE // SCAN_COLS):
        cols = pl.ds(cb * SCAN_COLS, SCAN_COLS)
        ar = jnp.broadcast_to(a_ref[0:1, cols], (BATCH, SCAN_COLS))
        ai = jnp.broadcast_to(a_ref[1:2, cols], (BATCH, SCAN_COLS))

        def step(k, carry):
            sr, si = carry
            t = (T_BLK - 1 - k) if reverse else k
            rows = pl.ds(pl.multiple_of(t * BATCH, BATCH), BATCH)
            nr = ar * sr - ai * si + xre_ref[rows, cols]
            ni = ar * si + ai * sr + xim_ref[rows, cols]
            xre_ref[rows, cols] = nr
            xim_ref[rows, cols] = ni
            return nr, ni

        sr, si = lax.fori_loop(0, T_BLK, step, (st_ref[0, :, cols], st_ref[1, :, cols]),
                               unroll=2)
        st_ref[0, :, cols] = sr
        st_ref[1, :, cols] = si

    half = N_STATE // 2
    for hf in range(2):
        xr = xre_ref[:, hf * half:(hf + 1) * half].astype(bf16)
        xi = xim_ref[:, hf * half:(hf + 1) * half].astype(bf16)
        y = (jnp.dot(xr, wc_ref[hf, :half, :], preferred_element_type=f32)
             + jnp.dot(xi, wc_ref[hf, half:, :], preferred_element_type=f32))
        y_ref[:, hf * (D_S5 // 2):(hf + 1) * (D_S5 // 2)] = y


def _s5_scan(u, wb, a, wc, reverse):
    rows = u.shape[0]
    nblk = rows // ROWS_BLK
    idx = (lambda i: (nblk - 1 - i, 0)) if reverse else (lambda i: (i, 0))
    return pl.pallas_call(
        functools.partial(_s5_kernel, reverse=reverse),
        out_shape=jax.ShapeDtypeStruct((rows, D_S5), f32),
        grid=(nblk,),
        in_specs=[pl.BlockSpec((ROWS_BLK, D_S5), idx),
                  _const_spec((N_PAIRS, LANES, PAIR_COLS)),
                  _const_spec((2, N_STATE)),
                  _const_spec((2, N_STATE, D_S5 // 2))],
        out_specs=pl.BlockSpec((ROWS_BLK, D_S5), idx),
        scratch_shapes=[pltpu.VMEM((ROWS_BLK, N_STATE), f32),
                        pltpu.VMEM((ROWS_BLK, N_STATE), f32),
                        pltpu.VMEM((2, BATCH, N_STATE), f32)],
        compiler_params=pltpu.CompilerParams(
            dimension_semantics=("arbitrary",), vmem_limit_bytes=VMEM_LIMIT),
        name="s5_bwd" if reverse else "s5_fwd",
    )(u, wb, a, wc)


def _mixout_kernel(x_ref, yf_ref, yb_ref, u_ref, ug_ref, v_ref, d_ref, wglu_ref,
                   wsp_ref, bsp_ref, gs5_ref, ggm_ref, wout_ref, gate_ref, o_ref):
    y = yf_ref[...] + yb_ref[...] + d_ref[...] * u_ref[...]
    y = jax.nn.gelu(y)
    y = y * jax.nn.sigmoid(jnp.dot(y.astype(bf16), wglu_ref[...], preferred_element_type=f32))
    s5n = _rms(y, gs5_ref[...]).astype(bf16)

    vb = v_ref[...].astype(bf16)
    mixed = jnp.concatenate(
        [jnp.dot(wsp_ref[h], vb[:, h * GMLP_HEAD_DIM:(h + 1) * GMLP_HEAD_DIM],
                 preferred_element_type=f32) for h in range(GMLP_HEADS)], axis=-1)
    ygm = ug_ref[...] * (mixed + bsp_ref[...])
    gmn = _rms(ygm, ggm_ref[...]).astype(bf16)

    proj = (jnp.dot(s5n, wout_ref[:D_S5, :], preferred_element_type=f32)
            + jnp.dot(gmn, wout_ref[D_S5:, :], preferred_element_type=f32))
    o_ref[...] = x_ref[...] + _per_batch(proj, gate_ref[...], jnp.multiply)


def _mixout(x, yf, yb, u, ug, v, d, w_glu, wsp, bsp, g_s5, g_gm, w_out, gate):
    rows = x.shape[0]
    full = pl.BlockSpec((ROWS_BLK, D_MODEL), lambda i: (i, 0))
    half = pl.BlockSpec((ROWS_BLK, D_S5), lambda i: (i, 0))
    return pl.pallas_call(
        _mixout_kernel,
        out_shape=jax.ShapeDtypeStruct((rows, D_MODEL), f32),
        grid=(rows // ROWS_BLK,),
        in_specs=[full, half, half, half, half, half,
                  _const_spec((1, D_S5)),
                  _const_spec((D_S5, D_S5)),
                  _const_spec((GMLP_HEADS, ROWS_BLK, ROWS_BLK)),
                  _const_spec((ROWS_BLK, D_GMLP)),
                  _const_spec((1, D_S5)),
                  _const_spec((1, D_GMLP)),
                  _const_spec((D_MODEL, D_MODEL)),
                  _const_spec((BATCH, D_MODEL))],
        out_specs=full,
        compiler_params=pltpu.CompilerParams(
            dimension_semantics=("parallel",), vmem_limit_bytes=VMEM_LIMIT),
        name="mixout",
    )(x, yf, yb, u, ug, v, d, w_glu, wsp, bsp, g_s5, g_gm, w_out, gate)


def _trunk(x, mod, p):
    bsz, seq, _ = x.shape
    assert bsz == BATCH and seq % T_BLK == 0
    xt = jnp.swapaxes(x, 0, 1).reshape(seq * BATCH, D_MODEL)
    mod = mod.reshape(BATCH, N_MOD, D_MODEL).transpose(1, 0, 2)
    x1 = _ffn(xt, mod[0:3], p["g1"], p["ffn1_in"], p["ffn1_out"])
    u, ug, v = _mixin(x1, mod[3:6], p["g_mix"], p["w_mix_in"], p["ln_g"], p["ln_b"])
    yf = _s5_scan(u, p["wb_f"], p["a_f"], p["wc_f"], reverse=False)
    yb = _s5_scan(u, p["wb_b"], p["a_b"], p["wc_b"], reverse=True)
    x2 = _mixout(x1, yf, yb, u, ug, v, p["d"], p["w_glu"], p["wsp"], p["bsp"],
                 p["g_s5"], p["g_gm"], p["w_mix_out"], mod[5])
    y = _ffn(x2, mod[6:9], p["g2"], p["ffn2_in"], p["ffn2_out"], final_g=p["g_final"])
    return jnp.swapaxes(y.reshape(seq, BATCH, D_MODEL), 0, 1)


def kernel(x_prompt, x_sample, c_prompt, c_sample, w_ada, b_ada, norm_ffn1_g, ffn1_w_in, ffn1_w_out, norm_mix_g, w_mix_in, s5_lam_re_f, s5_lam_im_f, s5_log_step_f, s5_b_re_f, s5_b_im_f, s5_c_re_f, s5_c_im_f, s5_lam_re_b, s5_lam_im_b, s5_log_step_b, s5_b_re_b, s5_b_im_b, s5_c_re_b, s5_c_im_b, s5_d, s5_w_glu, gmlp_ln_g, gmlp_ln_b, gmlp_w_sp, gmlp_b_sp, norm_out_s5_g, norm_out_gmlp_g, w_mix_out, norm_ffn2_g, ffn2_w_in, ffn2_w_out, final_norm_g):
    assert w_ada.shape[0] == 1, "single layer"
    row = lambda a: a.reshape(1, -1)
    both = lambda f, b: jnp.stack([f[0], b[0]])

    a_re, a_im, bb_re, bb_im = _s5_prep(
        both(s5_lam_re_f, s5_lam_re_b), both(s5_lam_im_f, s5_lam_im_b),
        both(s5_log_step_f, s5_log_step_b), both(s5_b_re_f, s5_b_re_b),
        both(s5_b_im_f, s5_b_im_b))

    eye_b = jnp.eye(BATCH, dtype=f32)
    wsp = jnp.einsum('hqk,ab->hqakb', gmlp_w_sp[0], eye_b)
    wsp = wsp.reshape(GMLP_HEADS, ROWS_BLK, ROWS_BLK).astype(bf16)
    bsp = jnp.broadcast_to(gmlp_b_sp[0].T[:, None, :, None],
                           (CHUNK, BATCH, GMLP_HEADS, GMLP_HEAD_DIM)).reshape(ROWS_BLK, D_GMLP)

    p = dict(
        g1=row(norm_ffn1_g), ffn1_in=ffn1_w_in[0].astype(bf16), ffn1_out=ffn1_w_out[0].astype(bf16),
        g_mix=row(norm_mix_g), w_mix_in=w_mix_in[0].astype(bf16),
        ln_g=row(gmlp_ln_g), ln_b=row(gmlp_ln_b),
        wb_f=_s5_b_tiles(bb_re[0], bb_im[0]), wb_b=_s5_b_tiles(bb_re[1], bb_im[1]),
        a_f=jnp.stack([a_re[0], a_im[0]]), a_b=jnp.stack([a_re[1], a_im[1]]),
        wc_f=_s5_c_tiles(s5_c_re_f[0], s5_c_im_f[0]), wc_b=_s5_c_tiles(s5_c_re_b[0], s5_c_im_b[0]),
        d=row(s5_d), w_glu=s5_w_glu[0].astype(bf16), wsp=wsp, bsp=bsp,
        g_s5=row(norm_out_s5_g), g_gm=row(norm_out_gmlp_g), w_mix_out=w_mix_out[0].astype(bf16),
        g2=row(norm_ffn2_g), ffn2_in=ffn2_w_in[0].astype(bf16), ffn2_out=ffn2_w_out[0].astype(bf16),
        g_final=row(final_norm_g),
    )

    c = jnp.concatenate([c_prompt, c_sample], axis=0)
    mod = _ada(c, w_ada[0], b_ada)
    nb = c_prompt.shape[0]
    return (_trunk(x_prompt, mod[:nb], p), _trunk(x_sample, mod[nb:], p))
```

```python
import functools

import jax
import jax.numpy as jnp
from jax import lax
from jax.experimental import pallas as pl
from jax.experimental.pallas import tpu as pltpu

D_MODEL = 1024
BATCH = 8
D_S5 = 512
S5_GROUPS = 32
S5_GROUP = 16
S5_STATE = 64
N_STATE = S5_GROUPS * S5_STATE
D_GMLP = 512
GMLP_HEADS = 4
GMLP_HEAD_DIM = 128
CHUNK = 128
D_FF = 2816
N_MOD = 9
EPS = 1e-6

LANES = 128
PAIR_COLS = 2 * LANES
N_PAIRS = S5_GROUPS // 2
VMEM_LIMIT = 56 * 1024 * 1024

T_FFN = 64
TM_FFN = T_FFN * BATCH
T_BLK = 128
ROWS_BLK = T_BLK * BATCH
HALF_ROWS = ROWS_BLK // 2
SCAN_COLS = 512

f32 = jnp.float32
bf16 = jnp.bfloat16


def _const_spec(shape):
    nd = len(shape)
    return pl.BlockSpec(shape, lambda i: (0,) * nd, pipeline_mode=pl.Buffered(1))


def _rms(x, g):
    ms = jnp.mean(x * x, axis=-1, keepdims=True)
    return x * lax.rsqrt(ms + EPS) * g


def _per_batch(x, v, op):
    rows, d = x.shape
    x3 = x.reshape(rows // BATCH, BATCH, d)
    return op(x3, v[None]).reshape(rows, d)


def _modulate(h, shift, scale):
    return _per_batch(_per_batch(h, 1.0 + scale, jnp.multiply), shift, jnp.add)


def _ada_kernel(c_ref, w_ref, b_ref, o_ref):
    c = c_ref[...]
    s = c * jax.nn.sigmoid(c)
    o_ref[...] = jnp.dot(s.astype(bf16), w_ref[...].astype(bf16),
                         preferred_element_type=f32) + b_ref[...]


def _ada(c, w, b):
    n = c.shape[0]
    return pl.pallas_call(
        _ada_kernel,
        out_shape=jax.ShapeDtypeStruct((n, N_MOD * D_MODEL), f32),
        grid=(N_MOD,),
        in_specs=[pl.BlockSpec((n, D_MODEL), lambda j: (0, 0)),
                  pl.BlockSpec((D_MODEL, D_MODEL), lambda j: (0, j)),
                  pl.BlockSpec((1, D_MODEL), lambda j: (0, j))],
        out_specs=pl.BlockSpec((n, D_MODEL), lambda j: (0, j)),
        compiler_params=pltpu.CompilerParams(dimension_semantics=("arbitrary",)),
        name="ada",
    )(c, w, b)


def _swiglu_step(x, mod_ref, g_ref, win_ref, wout_ref):
    h = _modulate(_rms(x, g_ref[...]), mod_ref[0], mod_ref[1]).astype(bf16)
    a = jnp.dot(h, win_ref[...], preferred_element_type=f32)
    act = (jax.nn.silu(a[:, :D_FF]) * a[:, D_FF:]).astype(bf16)
    f = jnp.dot(act, wout_ref[...], preferred_element_type=f32)
    return x + _per_batch(f, 0.5 * mod_ref[2], jnp.multiply)


def _ffn1_mixin_kernel(x_ref, mod1_ref, g1_ref, win_ref, wout_ref,
                       mod2_ref, g2_ref, wmix_ref, lng_ref, lnb_ref,
                       x1_ref, u_ref, ug_ref, v_ref):
    x = jnp.swapaxes(x_ref[...], 0, 1).reshape(TM_FFN, D_MODEL)
    x1 = _swiglu_step(x, mod1_ref, g1_ref, win_ref, wout_ref)
    x1_ref[...] = x1
    h = _modulate(_rms(x1, g2_ref[...]), mod2_ref[0], mod2_ref[1]).astype(bf16)
    z = jnp.dot(h, wmix_ref[...], preferred_element_type=f32)
    u_ref[...] = z[:, :D_S5]
    zg = jax.nn.gelu(z[:, D_S5:])
    ug_ref[...] = zg[:, :D_GMLP]
    v = zg[:, D_GMLP:]
    mu = jnp.mean(v, axis=-1, keepdims=True)
    vc = v - mu
    var = jnp.mean(vc * vc, axis=-1, keepdims=True)
    v_ref[...] = (vc * lax.rsqrt(var + EPS) * lng_ref[...] + lnb_ref[...]).astype(bf16)


def _ffn1_mixin(x, mod, p):
    seq = x.shape[1]
    rows = seq * BATCH
    row_blk = lambda w: pl.BlockSpec((TM_FFN, w), lambda i: (i, 0))
    return pl.pallas_call(
        _ffn1_mixin_kernel,
        out_shape=(jax.ShapeDtypeStruct((rows, D_MODEL), f32),
                   jax.ShapeDtypeStruct((rows, D_S5), f32),
                   jax.ShapeDtypeStruct((rows, D_GMLP), f32),
                   jax.ShapeDtypeStruct((rows, D_GMLP), bf16)),
        grid=(rows // TM_FFN,),
        in_specs=[pl.BlockSpec((BATCH, T_FFN, D_MODEL), lambda i: (0, i, 0)),
                  _const_spec((3, BATCH, D_MODEL)),
                  _const_spec((1, D_MODEL)),
                  _const_spec((D_MODEL, 2 * D_FF)),
                  _const_spec((D_FF, D_MODEL)),
                  _const_spec((3, BATCH, D_MODEL)),
                  _const_spec((1, D_MODEL)),
                  _const_spec((D_MODEL, D_S5 + 2 * D_GMLP)),
                  _const_spec((1, D_GMLP)),
                  _const_spec((1, D_GMLP))],
        out_specs=(row_blk(D_MODEL), row_blk(D_S5), row_blk(D_GMLP), row_blk(D_GMLP)),
        compiler_params=pltpu.CompilerParams(
            dimension_semantics=("parallel",), vmem_limit_bytes=VMEM_LIMIT),
        name="ffn1_mixin",
    )(x, mod[0:3], p["g1"], p["ffn1_in"], p["ffn1_out"],
      mod[3:6], p["g_mix"], p["w_mix_in"], p["ln_g"], p["ln_b"])


def _ffn2_final_kernel(x_ref, mod_ref, g_ref, win_ref, wout_ref, gf_ref, o_ref):
    y = _rms(_swiglu_step(x_ref[...], mod_ref, g_ref, win_ref, wout_ref), gf_ref[...])
    o_ref[...] = jnp.swapaxes(y.reshape(T_FFN, BATCH, D_MODEL), 0, 1)


def _ffn2_final(x, mod3, p):
    rows = x.shape[0]
    return pl.pallas_call(
        _ffn2_final_kernel,
        out_shape=jax.ShapeDtypeStruct((BATCH, rows // BATCH, D_MODEL), f32),
        grid=(rows // TM_FFN,),
        in_specs=[pl.BlockSpec((TM_FFN, D_MODEL), lambda i: (i, 0)),
                  _const_spec((3, BATCH, D_MODEL)),
                  _const_spec((1, D_MODEL)),
                  _const_spec((D_MODEL, 2 * D_FF)),
                  _const_spec((D_FF, D_MODEL)),
                  _const_spec((1, D_MODEL))],
        out_specs=pl.BlockSpec((BATCH, T_FFN, D_MODEL), lambda i: (0, i, 0)),
        compiler_params=pltpu.CompilerParams(
            dimension_semantics=("parallel",), vmem_limit_bytes=VMEM_LIMIT),
        name="ffn2_final",
    )(x, mod3, p["g2"], p["ffn2_in"], p["ffn2_out"], p["g_final"])


def _s5_prep_kernel(lre_ref, lim_ref, ls_ref, bre_ref, bim_ref, cre_ref, cim_ref, seg_ref,
                    a2_ref, bb_ref, abb_ref, ca_ref, cb_ref):
    lre = lre_ref[...]
    lim = lim_ref[...]
    dt = jnp.exp(ls_ref[...])
    mag = jnp.exp(lre * dt)
    ab_re = mag * jnp.cos(lim * dt)
    ab_im = mag * jnp.sin(lim * dt)
    n_re = ab_re - 1.0
    n_im = ab_im
    den = lre * lre + lim * lim
    f_re = (n_re * lre + n_im * lim) / den
    f_im = (n_im * lre - n_re * lim) / den
    a2_ref[0] = ab_re * ab_re - ab_im * ab_im
    a2_ref[1] = 2.0 * ab_re * ab_im

    bre = bre_ref[...]
    bim = bim_ref[...]
    bb_re = f_re[:, None, :] * bre - f_im[:, None, :] * bim
    bb_im = f_re[:, None, :] * bim + f_im[:, None, :] * bre
    bb_ref[0] = bb_re
    bb_ref[1] = bb_im
    ar = ab_re[:, None, :]
    ai = ab_im[:, None, :]
    abb_ref[0] = ar * bb_re - ai * bb_im
    abb_ref[1] = ar * bb_im + ai * bb_re

    cre = cre_ref[...]
    cim = cim_ref[...]
    ca_ref[0] = cre * ar - cim * ai
    ca_ref[1] = cre * ai + cim * ar

    prod = (cre[:, :, None, :] * bb_re[:, None, :, :]
            - cim[:, :, None, :] * bb_im[:, None, :, :])
    prod = prod.reshape(2 * S5_GROUP * S5_GROUP, N_STATE)
    cb_ref[...] = jnp.dot(prod, seg_ref[...], precision=lax.Precision.HIGHEST,
                          preferred_element_type=f32)


def _s5_prep(lam_re, lam_im, log_step, b_re, b_im, c_re, c_im):
    flat = lambda a: a.reshape(2, N_STATE)
    ls = jnp.broadcast_to(log_step[:, :, None], (2, S5_GROUPS, S5_STATE))
    bt = lambda b: b.transpose(0, 3, 1, 2).reshape(2, S5_GROUP, N_STATE)
    ct = lambda c: c.transpose(0, 2, 1, 3).reshape(2, S5_GROUP, N_STATE)
    seg = jnp.repeat(jnp.eye(S5_GROUPS, dtype=f32), S5_STATE, axis=0)
    vec = jax.ShapeDtypeStruct((2, 2, N_STATE), f32)
    mat = jax.ShapeDtypeStruct((2, 2, S5_GROUP, N_STATE), f32)
    cb = jax.ShapeDtypeStruct((2 * S5_GROUP * S5_GROUP, S5_GROUPS), f32)
    return pl.pallas_call(
        _s5_prep_kernel, out_shape=(vec, mat, mat, mat, cb), name="s5_prep",
    )(flat(lam_re), flat(lam_im), flat(ls), bt(b_re), bt(b_im), ct(c_re), ct(c_im), seg)


def _s5_b_tiles(b_re, b_im):
    def gph(b):
        return b.reshape(S5_GROUP, N_PAIRS, 2, S5_STATE).transpose(1, 2, 0, 3)
    both = jnp.stack([gph(b_re), gph(b_im)], axis=3)
    pair = jnp.einsum('ijhcp,jk->ijhckp', both, jnp.eye(2, dtype=f32))
    pair = pair.reshape(N_PAIRS, 2 * S5_GROUP, PAIR_COLS)
    pairs_per_tile = LANES // (2 * S5_GROUP)
    slot = jax.nn.one_hot(jnp.arange(N_PAIRS) % pairs_per_tile, pairs_per_tile, dtype=f32)
    w = jnp.einsum('irn,is->isrn', pair, slot)
    return w.reshape(N_PAIRS, LANES, PAIR_COLS)


def _s5_c_tiles(c_re, c_im):
    half_groups = S5_GROUPS // 2
    eye = jnp.eye(half_groups, dtype=f32)

    def bd(c):
        c = c.reshape(S5_GROUP, 2, half_groups, S5_STATE)
        m = jnp.einsum('hagp,gk->agpkh', c, eye)
        return m.reshape(2, half_groups * S5_STATE, half_groups * S5_GROUP)
    return jnp.concatenate([bd(c_re), -bd(c_im)], axis=1).astype(bf16)


def _s5_feedthrough(cb):
    cb = cb.reshape(S5_GROUP, S5_GROUP, S5_GROUPS)
    m = jnp.einsum('oig,gk->giko', cb, jnp.eye(S5_GROUPS, dtype=f32))
    return m.reshape(D_S5, D_S5).astype(bf16)


def _s5_weights(prep, c_re, c_im, d, reverse):
    a2, bb, abb, ca, cb = prep
    plain = _s5_b_tiles(bb[0, d], bb[1, d])
    stepped = _s5_b_tiles(abb[0, d], abb[1, d])
    first, second = (plain, stepped) if reverse else (stepped, plain)
    ct = lambda c: c.transpose(1, 0, 2).reshape(S5_GROUP, N_STATE)
    return dict(
        wb=jnp.concatenate([first, second], axis=1).astype(bf16),
        a2=a2[:, d],
        wc=_s5_c_tiles(ct(c_re), ct(c_im)),
        wca=_s5_c_tiles(ca[0, d], ca[1, d]),
        wcb=_s5_feedthrough(cb[d * S5_GROUP * S5_GROUP:(d + 1) * S5_GROUP * S5_GROUP]),
    )


def _s5_outputs(u_ref, wb_ref, a2_ref, wc_ref, wca_ref, wcb_ref, xre_ref, xim_ref, reverse):
    carry_row = HALF_ROWS if reverse else 0
    own_row = 0 if reverse else BATCH
    prev_row = BATCH - own_row

    @pl.when(pl.program_id(0) == 0)
    def _():
        zeros = jnp.zeros((BATCH, N_STATE), f32)
        xre_ref[carry_row:carry_row + BATCH, :] = zeros
        xim_ref[carry_row:carry_row + BATCH, :] = zeros

    u4 = u_ref[...].reshape(T_BLK // 2, 2, BATCH, D_S5)
    u_even = u4[:, 0].reshape(HALF_ROWS, D_S5).astype(bf16)
    u_odd = u4[:, 1].reshape(HALF_ROWS, D_S5).astype(bf16)

    for i in range(N_PAIRS):
        lane0 = (i * 2 * S5_GROUP // LANES) * LANES
        lhs = jnp.concatenate([u_even[:, lane0:lane0 + LANES], u_odd[:, lane0:lane0 + LANES]], axis=1)
        r = jnp.dot(lhs, wb_ref[i], preferred_element_type=f32)
        xre_ref[own_row:own_row + HALF_ROWS, i * LANES:(i + 1) * LANES] = r[:, :LANES]
        xim_ref[own_row:own_row + HALF_ROWS, i * LANES:(i + 1) * LANES] = r[:, LANES:]

    n_steps = T_BLK // 2
    for cb in range(N_STATE // SCAN_COLS):
        cols = pl.ds(cb * SCAN_COLS, SCAN_COLS)
        ar = jnp.broadcast_to(a2_ref[0:1, cols], (BATCH, SCAN_COLS))
        ai = jnp.broadcast_to(a2_ref[1:2, cols], (BATCH, SCAN_COLS))

        def step(k, carry):
            sr, si = carry
            kk = (n_steps - 1 - k) if reverse else k
            rows = pl.ds(pl.multiple_of(own_row + kk * BATCH, BATCH), BATCH)
            nr = ar * sr - ai * si + xre_ref[rows, cols]
            ni = ar * si + ai * sr + xim_ref[rows, cols]
            xre_ref[rows, cols] = nr
            xim_ref[rows, cols] = ni
            return nr, ni

        init = (xre_ref[carry_row:carry_row + BATCH, cols], xim_ref[carry_row:carry_row + BATCH, cols])
        lax.fori_loop(0, n_steps, step, init, unroll=2)

    half = N_STATE // 2
    width = D_S5 // 2
    u_skip = u_odd if reverse else u_even
    y_own, y_skip = [], []
    for hf in range(2):
        def proj(row0, w_ref):
            xr = xre_ref[row0:row0 + HALF_ROWS, hf * half:(hf + 1) * half].astype(bf16)
            xi = xim_ref[row0:row0 + HALF_ROWS, hf * half:(hf + 1) * half].astype(bf16)
            return (jnp.dot(xr, w_ref[hf, :half, :], preferred_element_type=f32)
                    + jnp.dot(xi, w_ref[hf, half:, :], preferred_element_type=f32))
        y_own.append(proj(own_row, wc_ref))
        y_skip.append(proj(prev_row, wca_ref)
                      + jnp.dot(u_skip, wcb_ref[:, hf * width:(hf + 1) * width],
                                preferred_element_type=f32))
    y_own = jnp.concatenate(y_own, axis=1).reshape(T_BLK // 2, 1, BATCH, D_S5)
    y_skip = jnp.concatenate(y_skip, axis=1).reshape(T_BLK // 2, 1, BATCH, D_S5)
    pair = (y_own, y_skip) if reverse else (y_skip, y_own)
    y = jnp.concatenate(pair, axis=1).reshape(ROWS_BLK, D_S5)

    last_row = 0 if reverse else HALF_ROWS
    xre_ref[carry_row:carry_row + BATCH, :] = xre_ref[last_row:last_row + BATCH, :]
    xim_ref[carry_row:carry_row + BATCH, :] = xim_ref[last_row:last_row + BATCH, :]
    return y


def _s5_fwd_kernel(u_ref, wb_ref, a2_ref, wc_ref, wca_ref, wcb_ref, y_ref, xre_ref, xim_ref):
    y_ref[...] = _s5_outputs(u_ref, wb_ref, a2_ref, wc_ref, wca_ref, wcb_ref,
                             xre_ref, xim_ref, reverse=False)


def _s5_bwd_kernel(u_ref, wb_ref, a2_ref, wc_ref, wca_ref, wcb_ref, yf_ref, d_ref, wglu_ref,
                   g_ref, o_ref, xre_ref, xim_ref):
    yb = _s5_outputs(u_ref, wb_ref, a2_ref, wc_ref, wca_ref, wcb_ref,
                     xre_ref, xim_ref, reverse=True)
    y = jax.nn.gelu(yf_ref[...] + yb + d_ref[...] * u_ref[...])
    y = y * jax.nn.sigmoid(jnp.dot(y.astype(bf16), wglu_ref[...], preferred_element_type=f32))
    o_ref[...] = _rms(y, g_ref[...]).astype(bf16)


def _s5_specs(idx):
    return [pl.BlockSpec((ROWS_BLK, D_S5), idx),
            _const_spec((N_PAIRS, PAIR_COLS, PAIR_COLS)),
            _const_spec((2, N_STATE)),
            _const_spec((2, N_STATE, D_S5 // 2)),
            _const_spec((2, N_STATE, D_S5 // 2)),
            _const_spec((D_S5, D_S5))]


_S5_SCRATCH = [pltpu.VMEM((HALF_ROWS + BATCH, N_STATE), f32),
               pltpu.VMEM((HALF_ROWS + BATCH, N_STATE), f32)]
_S5_PARAMS = pltpu.CompilerParams(dimension_semantics=("arbitrary",), vmem_limit_bytes=VMEM_LIMIT)


def _s5_fwd(u, w):
    rows = u.shape[0]
    idx = lambda i: (i, 0)
    return pl.pallas_call(
        _s5_fwd_kernel,
        out_shape=jax.ShapeDtypeStruct((rows, D_S5), f32),
        grid=(rows // ROWS_BLK,),
        in_specs=_s5_specs(idx),
        out_specs=pl.BlockSpec((ROWS_BLK, D_S5), idx),
        scratch_shapes=_S5_SCRATCH,
        compiler_params=_S5_PARAMS,
        name="s5_fwd",
    )(u, w["wb"], w["a2"], w["wc"], w["wca"], w["wcb"])


def _s5_bwd_glu(u, yf, w, d, w_glu, g_s5):
    rows = u.shape[0]
    nblk = rows // ROWS_BLK
    idx = lambda i: (nblk - 1 - i, 0)
    return pl.pallas_call(
        _s5_bwd_kernel,
        out_shape=jax.ShapeDtypeStruct((rows, D_S5), bf16),
        grid=(nblk,),
        in_specs=_s5_specs(idx) + [pl.BlockSpec((ROWS_BLK, D_S5), idx),
                                   _const_spec((1, D_S5)),
                                   _const_spec((D_S5, D_S5)),
                                   _const_spec((1, D_S5))],
        out_specs=pl.BlockSpec((ROWS_BLK, D_S5), idx),
        scratch_shapes=_S5_SCRATCH,
        compiler_params=_S5_PARAMS,
        name="s5_bwd_glu",
    )(u, w["wb"], w["a2"], w["wc"], w["wca"], w["wcb"], yf, d, w_glu, g_s5)


def _mixout_kernel(x_ref, s5n_ref, ug_ref, v_ref, wsp_ref, bsp_ref, ggm_ref, wout_ref,
                   gate_ref, o_ref):
    vb = v_ref[...]
    mixed = jnp.concatenate(
        [jnp.dot(wsp_ref[h], vb[:, h * GMLP_HEAD_DIM:(h + 1) * GMLP_HEAD_DIM],
                 preferred_element_type=f32) for h in range(GMLP_HEADS)], axis=-1)
    ygm = ug_ref[...] * (mixed + bsp_ref[...])
    gmn = _rms(ygm, ggm_ref[...]).astype(bf16)
    proj = (jnp.dot(s5n_ref[...], wout_ref[:D_S5, :], preferred_element_type=f32)
            + jnp.dot(gmn, wout_ref[D_S5:, :], preferred_element_type=f32))
    o_ref[...] = x_ref[...] + _per_batch(proj, gate_ref[...], jnp.multiply)


def _mixout(x, s5n, ug, v, wsp, bsp, g_gm, w_out, gate):
    rows = x.shape[0]
    full = pl.BlockSpec((ROWS_BLK, D_MODEL), lambda i: (i, 0))
    half = pl.BlockSpec((ROWS_BLK, D_S5), lambda i: (i, 0))
    return pl.pallas_call(
        _mixout_kernel,
        out_shape=jax.ShapeDtypeStruct((rows, D_MODEL), f32),
        grid=(rows // ROWS_BLK,),
        in_specs=[full, half, half, half,
                  _const_spec((GMLP_HEADS, ROWS_BLK, ROWS_BLK)),
                  _const_spec((ROWS_BLK, D_GMLP)),
                  _const_spec((1, D_GMLP)),
                  _const_spec((D_MODEL, D_MODEL)),
                  _const_spec((BATCH, D_MODEL))],
        out_specs=full,
        compiler_params=pltpu.CompilerParams(
            dimension_semantics=("parallel",), vmem_limit_bytes=VMEM_LIMIT),
        name="mixout",
    )(x, s5n, ug, v, wsp, bsp, g_gm, w_out, gate)


def _gmlp_spatial_weights(w_sp, b_sp):
    expand = jnp.repeat(jnp.eye(CHUNK, dtype=f32), BATCH, axis=0)
    big = jnp.einsum('rq,hqk,ck->hrc', expand, w_sp, expand)
    r = jnp.arange(ROWS_BLK) % BATCH
    same_batch = (r[:, None] == r[None, :]).astype(f32)
    wsp = (big * same_batch[None]).astype(bf16)
    bsp = jnp.broadcast_to(b_sp.T[:, None, :, None],
                           (CHUNK, BATCH, GMLP_HEADS, GMLP_HEAD_DIM)).reshape(ROWS_BLK, D_GMLP)
    return wsp, bsp


def _trunk(x, mod, p):
    bsz, seq, _ = x.shape
    assert bsz == BATCH and seq % T_BLK == 0 and CHUNK == T_BLK
    mod = mod.reshape(BATCH, N_MOD, D_MODEL).transpose(1, 0, 2)
    x1, u, ug, v = _ffn1_mixin(x, mod, p)
    yf = _s5_fwd(u, p["s5_f"])
    s5n = _s5_bwd_glu(u, yf, p["s5_b"], p["d"], p["w_glu"], p["g_s5"])
    x2 = _mixout(x1, s5n, ug, v, p["wsp"], p["bsp"], p["g_gm"], p["w_mix_out"], mod[5])
    return _ffn2_final(x2, mod[6:9], p)


def kernel(x_prompt, x_sample, c_prompt, c_sample, w_ada, b_ada, norm_ffn1_g, ffn1_w_in, ffn1_w_out, norm_mix_g, w_mix_in, s5_lam_re_f, s5_lam_im_f, s5_log_step_f, s5_b_re_f, s5_b_im_f, s5_c_re_f, s5_c_im_f, s5_lam_re_b, s5_lam_im_b, s5_log_step_b, s5_b_re_b, s5_b_im_b, s5_c_re_b, s5_c_im_b, s5_d, s5_w_glu, gmlp_ln_g, gmlp_ln_b, gmlp_w_sp, gmlp_b_sp, norm_out_s5_g, norm_out_gmlp_g, w_mix_out, norm_ffn2_g, ffn2_w_in, ffn2_w_out, final_norm_g):
    assert w_ada.shape[0] == 1, "single layer"
    row = lambda a: a.reshape(1, -1)
    both = lambda f, b: jnp.stack([f[0], b[0]])

    prep = _s5_prep(
        both(s5_lam_re_f, s5_lam_re_b), both(s5_lam_im_f, s5_lam_im_b),
        both(s5_log_step_f, s5_log_step_b), both(s5_b_re_f, s5_b_re_b),
        both(s5_b_im_f, s5_b_im_b), both(s5_c_re_f, s5_c_re_b), both(s5_c_im_f, s5_c_im_b))
    wsp, bsp = _gmlp_spatial_weights(gmlp_w_sp[0], gmlp_b_sp[0])

    p = dict(
        g1=row(norm_ffn1_g), ffn1_in=ffn1_w_in[0].astype(bf16), ffn1_out=ffn1_w_out[0].astype(bf16),
        g_mix=row(norm_mix_g), w_mix_in=w_mix_in[0].astype(bf16),
        ln_g=row(gmlp_ln_g), ln_b=row(gmlp_ln_b),
        s5_f=_s5_weights(prep, s5_c_re_f[0], s5_c_im_f[0], 0, reverse=False),
        s5_b=_s5_weights(prep, s5_c_re_b[0], s5_c_im_b[0], 1, reverse=True),
        d=row(s5_d), w_glu=s5_w_glu[0].astype(bf16), wsp=wsp, bsp=bsp,
        g_s5=row(norm_out_s5_g), g_gm=row(norm_out_gmlp_g), w_mix_out=w_mix_out[0].astype(bf16),
        g2=row(norm_ffn2_g), ffn2_in=ffn2_w_in[0].astype(bf16), ffn2_out=ffn2_w_out[0].astype(bf16),
        g_final=row(final_norm_g),
    )

    c = jnp.concatenate([c_prompt, c_sample], axis=0)
    mod = _ada(c, w_ada[0], b_ada)
    nb = c_prompt.shape[0]
    return (_trunk(x_prompt, mod[:nb], p), _trunk(x_sample, mod[nb:], p))
```

```python
import functools

import jax
import jax.numpy as jnp
from jax import lax
from jax.experimental import pallas as pl
from jax.experimental.pallas import tpu as pltpu

D_MODEL = 1024
BATCH = 8
D_S5 = 512
S5_GROUPS = 32
S5_GROUP = 16
S5_STATE = 64
N_STATE = S5_GROUPS * S5_STATE
D_GMLP = 512
GMLP_HEADS = 4
GMLP_HEAD_DIM = 128
CHUNK = 128
D_FF = 2816
N_MOD = 9
EPS = 1e-6

LANES = 128
PAIR_COLS = 2 * LANES
N_PAIRS = S5_GROUPS // 2
VMEM_LIMIT = 56 * 1024 * 1024

T_FFN = 64
TM_FFN = T_FFN * BATCH
FFN_SUBTILES = 2
T_BLK = 128
ROWS_BLK = T_BLK * BATCH
HALF_ROWS = ROWS_BLK // 2
SCAN_COLS = 512

f32 = jnp.float32
bf16 = jnp.bfloat16


def _const_spec(shape):
    nd = len(shape)
    return pl.BlockSpec(shape, lambda i: (0,) * nd, pipeline_mode=pl.Buffered(1))


def _rms(x, g):
    ms = jnp.mean(x * x, axis=-1, keepdims=True)
    return x * lax.rsqrt(ms + EPS) * g


def _per_batch(x, v, op):
    rows, d = x.shape
    x3 = x.reshape(rows // BATCH, BATCH, d)
    return op(x3, v[None]).reshape(rows, d)


def _modulate(h, shift, scale):
    return _per_batch(_per_batch(h, 1.0 + scale, jnp.multiply), shift, jnp.add)


def _ada_kernel(c_ref, w_ref, b_ref, o_ref):
    c = c_ref[...]
    s = c * jax.nn.sigmoid(c)
    o_ref[...] = jnp.dot(s.astype(bf16), w_ref[...].astype(bf16),
                         preferred_element_type=f32) + b_ref[...]


def _ada(c, w, b):
    n = c.shape[0]
    return pl.pallas_call(
        _ada_kernel,
        out_shape=jax.ShapeDtypeStruct((n, N_MOD * D_MODEL), f32),
        grid=(N_MOD,),
        in_specs=[pl.BlockSpec((n, D_MODEL), lambda j: (0, 0)),
                  pl.BlockSpec((D_MODEL, D_MODEL), lambda j: (0, j)),
                  pl.BlockSpec((1, D_MODEL), lambda j: (0, j))],
        out_specs=pl.BlockSpec((n, D_MODEL), lambda j: (0, j)),
        compiler_params=pltpu.CompilerParams(dimension_semantics=("arbitrary",)),
        name="ada",
    )(c, w, b)


def _swiglu_step(x, mod_ref, g_ref, win_ref, wout_ref):
    h = _modulate(_rms(x, g_ref[...]), mod_ref[0], mod_ref[1]).astype(bf16)
    a = jnp.dot(h, win_ref[...], preferred_element_type=f32)
    act = (jax.nn.silu(a[:, :D_FF]) * a[:, D_FF:]).astype(bf16)
    f = jnp.dot(act, wout_ref[...], preferred_element_type=f32)
    return x + _per_batch(f, 0.5 * mod_ref[2], jnp.multiply)


def _ffn1_mixin_kernel(x_ref, mod1_ref, g1_ref, win_ref, wout_ref,
                       mod2_ref, g2_ref, wmix_ref, lng_ref, lnb_ref,
                       x1_ref, u_ref, ug_ref, v_ref):
    sub_t = T_FFN // FFN_SUBTILES
    sub_rows = sub_t * BATCH
    for s in range(FFN_SUBTILES):
        rows = slice(s * sub_rows, (s + 1) * sub_rows)
        x = jnp.swapaxes(x_ref[:, s * sub_t:(s + 1) * sub_t, :], 0, 1).reshape(sub_rows, D_MODEL)
        x1 = _swiglu_step(x, mod1_ref, g1_ref, win_ref, wout_ref)
        x1_ref[rows, :] = x1
        h = _modulate(_rms(x1, g2_ref[...]), mod2_ref[0], mod2_ref[1]).astype(bf16)
        z = jnp.dot(h, wmix_ref[...], preferred_element_type=f32)
        u_ref[rows, :] = z[:, :D_S5]
        zg = jax.nn.gelu(z[:, D_S5:])
        ug_ref[rows, :] = zg[:, :D_GMLP]
        v = zg[:, D_GMLP:]
        mu = jnp.mean(v, axis=-1, keepdims=True)
        vc = v - mu
        var = jnp.mean(vc * vc, axis=-1, keepdims=True)
        v_ref[rows, :] = (vc * lax.rsqrt(var + EPS) * lng_ref[...] + lnb_ref[...]).astype(bf16)


def _ffn1_mixin(x, mod, p):
    seq = x.shape[1]
    rows = seq * BATCH
    row_blk = lambda w: pl.BlockSpec((TM_FFN, w), lambda i: (i, 0))
    return pl.pallas_call(
        _ffn1_mixin_kernel,
        out_shape=(jax.ShapeDtypeStruct((rows, D_MODEL), f32),
                   jax.ShapeDtypeStruct((rows, D_S5), f32),
                   jax.ShapeDtypeStruct((rows, D_GMLP), f32),
                   jax.ShapeDtypeStruct((rows, D_GMLP), bf16)),
        grid=(rows // TM_FFN,),
        in_specs=[pl.BlockSpec((BATCH, T_FFN, D_MODEL), lambda i: (0, i, 0)),
                  _const_spec((3, BATCH, D_MODEL)),
                  _const_spec((1, D_MODEL)),
                  _const_spec((D_MODEL, 2 * D_FF)),
                  _const_spec((D_FF, D_MODEL)),
                  _const_spec((3, BATCH, D_MODEL)),
                  _const_spec((1, D_MODEL)),
                  _const_spec((D_MODEL, D_S5 + 2 * D_GMLP)),
                  _const_spec((1, D_GMLP)),
                  _const_spec((1, D_GMLP))],
        out_specs=(row_blk(D_MODEL), row_blk(D_S5), row_blk(D_GMLP), row_blk(D_GMLP)),
        compiler_params=pltpu.CompilerParams(
            dimension_semantics=("parallel",), vmem_limit_bytes=VMEM_LIMIT),
        name="ffn1_mixin",
    )(x, mod[0:3], p["g1"], p["ffn1_in"], p["ffn1_out"],
      mod[3:6], p["g_mix"], p["w_mix_in"], p["ln_g"], p["ln_b"])


def _ffn2_final_kernel(x_ref, mod_ref, g_ref, win_ref, wout_ref, gf_ref, o_ref):
    sub_t = T_FFN // FFN_SUBTILES
    sub_rows = sub_t * BATCH
    for s in range(FFN_SUBTILES):
        x = x_ref[s * sub_rows:(s + 1) * sub_rows, :]
        y = _rms(_swiglu_step(x, mod_ref, g_ref, win_ref, wout_ref), gf_ref[...])
        o_ref[:, s * sub_t:(s + 1) * sub_t, :] = jnp.swapaxes(y.reshape(sub_t, BATCH, D_MODEL), 0, 1)


def _ffn2_final(x, mod3, p):
    rows = x.shape[0]
    return pl.pallas_call(
        _ffn2_final_kernel,
        out_shape=jax.ShapeDtypeStruct((BATCH, rows // BATCH, D_MODEL), f32),
        grid=(rows // TM_FFN,),
        in_specs=[pl.BlockSpec((TM_FFN, D_MODEL), lambda i: (i, 0)),
                  _const_spec((3, BATCH, D_MODEL)),
                  _const_spec((1, D_MODEL)),
                  _const_spec((D_MODEL, 2 * D_FF)),
                  _const_spec((D_FF, D_MODEL)),
                  _const_spec((1, D_MODEL))],
        out_specs=pl.BlockSpec((BATCH, T_FFN, D_MODEL), lambda i: (0, i, 0)),
        compiler_params=pltpu.CompilerParams(
            dimension_semantics=("parallel",), vmem_limit_bytes=VMEM_LIMIT),
        name="ffn2_final",
    )(x, mod3, p["g2"], p["ffn2_in"], p["ffn2_out"], p["g_final"])


def _s5_prep_kernel(lre_ref, lim_ref, ls_ref, bre_ref, bim_ref, cre_ref, cim_ref, seg_ref,
                    a2_ref, bb_ref, abb_ref, ca_ref, cb_ref):
    lre = lre_ref[...]
    lim = lim_ref[...]
    dt = jnp.exp(ls_ref[...])
    mag = jnp.exp(lre * dt)
    ab_re = mag * jnp.cos(lim * dt)
    ab_im = mag * jnp.sin(lim * dt)
    n_re = ab_re - 1.0
    n_im = ab_im
    den = lre * lre + lim * lim
    f_re = (n_re * lre + n_im * lim) / den
    f_im = (n_im * lre - n_re * lim) / den
    a2_ref[0] = ab_re * ab_re - ab_im * ab_im
    a2_ref[1] = 2.0 * ab_re * ab_im

    bre = bre_ref[...]
    bim = bim_ref[...]
    bb_re = f_re[:, None, :] * bre - f_im[:, None, :] * bim
    bb_im = f_re[:, None, :] * bim + f_im[:, None, :] * bre
    bb_ref[0] = bb_re
    bb_ref[1] = bb_im
    ar = ab_re[:, None, :]
    ai = ab_im[:, None, :]
    abb_ref[0] = ar * bb_re - ai * bb_im
    abb_ref[1] = ar * bb_im + ai * bb_re

    cre = cre_ref[...]
    cim = cim_ref[...]
    ca_ref[0] = cre * ar - cim * ai
    ca_ref[1] = cre * ai + cim * ar

    prod = (cre[:, :, None, :] * bb_re[:, None, :, :]
            - cim[:, :, None, :] * bb_im[:, None, :, :])
    prod = prod.reshape(2 * S5_GROUP * S5_GROUP, N_STATE)
    cb_ref[...] = jnp.dot(prod, seg_ref[...], precision=lax.Precision.HIGHEST,
                          preferred_element_type=f32)


def _s5_prep(lam_re, lam_im, log_step, b_re, b_im, c_re, c_im):
    flat = lambda a: a.reshape(2, N_STATE)
    ls = jnp.broadcast_to(log_step[:, :, None], (2, S5_GROUPS, S5_STATE))
    bt = lambda b: b.transpose(0, 3, 1, 2).reshape(2, S5_GROUP, N_STATE)
    ct = lambda c: c.transpose(0, 2, 1, 3).reshape(2, S5_GROUP, N_STATE)
    seg = jnp.repeat(jnp.eye(S5_GROUPS, dtype=f32), S5_STATE, axis=0)
    vec = jax.ShapeDtypeStruct((2, 2, N_STATE), f32)
    mat = jax.ShapeDtypeStruct((2, 2, S5_GROUP, N_STATE), f32)
    cb = jax.ShapeDtypeStruct((2 * S5_GROUP * S5_GROUP, S5_GROUPS), f32)
    return pl.pallas_call(
        _s5_prep_kernel, out_shape=(vec, mat, mat, mat, cb), name="s5_prep",
    )(flat(lam_re), flat(lam_im), flat(ls), bt(b_re), bt(b_im), ct(c_re), ct(c_im), seg)


def _s5_b_tiles(b_re, b_im):
    def gph(b):
        return b.reshape(S5_GROUP, N_PAIRS, 2, S5_STATE).transpose(1, 2, 0, 3)
    both = jnp.stack([gph(b_re), gph(b_im)], axis=3)
    pair = jnp.einsum('ijhcp,jk->ijhckp', both, jnp.eye(2, dtype=f32))
    pair = pair.reshape(N_PAIRS, 2 * S5_GROUP, PAIR_COLS)
    pairs_per_tile = LANES // (2 * S5_GROUP)
    slot = jax.nn.one_hot(jnp.arange(N_PAIRS) % pairs_per_tile, pairs_per_tile, dtype=f32)
    w = jnp.einsum('irn,is->isrn', pair, slot)
    return w.reshape(N_PAIRS, LANES, PAIR_COLS)


def _s5_c_tiles(c_re, c_im):
    half_groups = S5_GROUPS // 2
    eye = jnp.eye(half_groups, dtype=f32)

    def bd(c):
        c = c.reshape(S5_GROUP, 2, half_groups, S5_STATE)
        m = jnp.einsum('hagp,gk->agpkh', c, eye)
        return m.reshape(2, half_groups * S5_STATE, half_groups * S5_GROUP)
    return jnp.concatenate([bd(c_re), -bd(c_im)], axis=1).astype(bf16)


def _s5_feedthrough(cb):
    cb = cb.reshape(S5_GROUP, S5_GROUP, S5_GROUPS)
    m = jnp.einsum('oig,gk->giko', cb, jnp.eye(S5_GROUPS, dtype=f32))
    return m.reshape(D_S5, D_S5).astype(bf16)


def _s5_weights(prep, c_re, c_im, d, reverse):
    a2, bb, abb, ca, cb = prep
    plain = _s5_b_tiles(bb[0, d], bb[1, d])
    stepped = _s5_b_tiles(abb[0, d], abb[1, d])
    first, second = (plain, stepped) if reverse else (stepped, plain)
    ct = lambda c: c.transpose(1, 0, 2).reshape(S5_GROUP, N_STATE)
    return dict(
        wb=jnp.concatenate([first, second], axis=1).astype(bf16),
        a2=a2[:, d],
        wc=_s5_c_tiles(ct(c_re), ct(c_im)),
        wca=_s5_c_tiles(ca[0, d], ca[1, d]),
        wcb=_s5_feedthrough(cb[d * S5_GROUP * S5_GROUP:(d + 1) * S5_GROUP * S5_GROUP]),
    )


def _s5_outputs(u_ref, wb_ref, a2_ref, wc_ref, wca_ref, wcb_ref, xre_ref, xim_ref, reverse):
    carry_row = HALF_ROWS if reverse else 0
    own_row = 0 if reverse else BATCH
    prev_row = BATCH - own_row

    @pl.when(pl.program_id(0) == 0)
    def _():
        zeros = jnp.zeros((BATCH, N_STATE), f32)
        xre_ref[carry_row:carry_row + BATCH, :] = zeros
        xim_ref[carry_row:carry_row + BATCH, :] = zeros

    u4 = u_ref[...].reshape(T_BLK // 2, 2, BATCH, D_S5)
    u_even = u4[:, 0].reshape(HALF_ROWS, D_S5).astype(bf16)
    u_odd = u4[:, 1].reshape(HALF_ROWS, D_S5).astype(bf16)

    for i in range(N_PAIRS):
        lane0 = (i * 2 * S5_GROUP // LANES) * LANES
        lhs = jnp.concatenate([u_even[:, lane0:lane0 + LANES], u_odd[:, lane0:lane0 + LANES]], axis=1)
        r = jnp.dot(lhs, wb_ref[i], preferred_element_type=f32)
        xre_ref[own_row:own_row + HALF_ROWS, i * LANES:(i + 1) * LANES] = r[:, :LANES]
        xim_ref[own_row:own_row + HALF_ROWS, i * LANES:(i + 1) * LANES] = r[:, LANES:]

    n_steps = T_BLK // 2
    for cb in range(N_STATE // SCAN_COLS):
        cols = pl.ds(cb * SCAN_COLS, SCAN_COLS)
        ar = jnp.broadcast_to(a2_ref[0:1, cols], (BATCH, SCAN_COLS))
        ai = jnp.broadcast_to(a2_ref[1:2, cols], (BATCH, SCAN_COLS))

        def step(k, carry):
            sr, si = carry
            kk = (n_steps - 1 - k) if reverse else k
            rows = pl.ds(pl.multiple_of(own_row + kk * BATCH, BATCH), BATCH)
            nr = ar * sr - ai * si + xre_ref[rows, cols]
            ni = ar * si + ai * sr + xim_ref[rows, cols]
            xre_ref[rows, cols] = nr
            xim_ref[rows, cols] = ni
            return nr, ni

        init = (xre_ref[carry_row:carry_row + BATCH, cols], xim_ref[carry_row:carry_row + BATCH, cols])
        lax.fori_loop(0, n_steps, step, init, unroll=True)

    half = N_STATE // 2
    width = D_S5 // 2
    u_skip = u_odd if reverse else u_even
    y_own, y_skip = [], []
    for hf in range(2):
        def proj(row0, w_ref):
            xr = xre_ref[row0:row0 + HALF_ROWS, hf * half:(hf + 1) * half].astype(bf16)
            xi = xim_ref[row0:row0 + HALF_ROWS, hf * half:(hf + 1) * half].astype(bf16)
            return (jnp.dot(xr, w_ref[hf, :half, :], preferred_element_type=f32)
                    + jnp.dot(xi, w_ref[hf, half:, :], preferred_element_type=f32))
        y_own.append(proj(own_row, wc_ref))
        y_skip.append(proj(prev_row, wca_ref)
                      + jnp.dot(u_skip, wcb_ref[:, hf * width:(hf + 1) * width],
                                preferred_element_type=f32))
    y_own = jnp.concatenate(y_own, axis=1).reshape(T_BLK // 2, 1, BATCH, D_S5)
    y_skip = jnp.concatenate(y_skip, axis=1).reshape(T_BLK // 2, 1, BATCH, D_S5)
    pair = (y_own, y_skip) if reverse else (y_skip, y_own)
    y = jnp.concatenate(pair, axis=1).reshape(ROWS_BLK, D_S5)

    last_row = 0 if reverse else HALF_ROWS
    xre_ref[carry_row:carry_row + BATCH, :] = xre_ref[last_row:last_row + BATCH, :]
    xim_ref[carry_row:carry_row + BATCH, :] = xim_ref[last_row:last_row + BATCH, :]
    return y


def _s5_fwd_kernel(u_ref, wb_ref, a2_ref, wc_ref, wca_ref, wcb_ref, y_ref, xre_ref, xim_ref):
    y_ref[...] = _s5_outputs(u_ref, wb_ref, a2_ref, wc_ref, wca_ref, wcb_ref,
                             xre_ref, xim_ref, reverse=False)


def _s5_bwd_kernel(u_ref, wb_ref, a2_ref, wc_ref, wca_ref, wcb_ref, yf_ref, d_ref, wglu_ref,
                   g_ref, o_ref, xre_ref, xim_ref):
    yb = _s5_outputs(u_ref, wb_ref, a2_ref, wc_ref, wca_ref, wcb_ref,
                     xre_ref, xim_ref, reverse=True)
    y = jax.nn.gelu(yf_ref[...] + yb + d_ref[...] * u_ref[...])
    y = y * jax.nn.sigmoid(jnp.dot(y.astype(bf16), wglu_ref[...], preferred_element_type=f32))
    o_ref[...] = _rms(y, g_ref[...]).astype(bf16)


def _s5_specs(idx):
    return [pl.BlockSpec((ROWS_BLK, D_S5), idx),
            _const_spec((N_PAIRS, PAIR_COLS, PAIR_COLS)),
            _const_spec((2, N_STATE)),
            _const_spec((2, N_STATE, D_S5 // 2)),
            _const_spec((2, N_STATE, D_S5 // 2)),
            _const_spec((D_S5, D_S5))]


_S5_SCRATCH = [pltpu.VMEM((HALF_ROWS + BATCH, N_STATE), f32),
               pltpu.VMEM((HALF_ROWS + BATCH, N_STATE), f32)]
_S5_PARAMS = pltpu.CompilerParams(dimension_semantics=("arbitrary",), vmem_limit_bytes=VMEM_LIMIT)


def _s5_fwd(u, w):
    rows = u.shape[0]
    idx = lambda i: (i, 0)
    return pl.pallas_call(
        _s5_fwd_kernel,
        out_shape=jax.ShapeDtypeStruct((rows, D_S5), f32),
        grid=(rows // ROWS_BLK,),
        in_specs=_s5_specs(idx),
        out_specs=pl.BlockSpec((ROWS_BLK, D_S5), idx),
        scratch_shapes=_S5_SCRATCH,
        compiler_params=_S5_PARAMS,
        name="s5_fwd",
    )(u, w["wb"], w["a2"], w["wc"], w["wca"], w["wcb"])


def _s5_bwd_glu(u, yf, w, d, w_glu, g_s5):
    rows = u.shape[0]
    nblk = rows // ROWS_BLK
    idx = lambda i: (nblk - 1 - i, 0)
    return pl.pallas_call(
        _s5_bwd_kernel,
        out_shape=jax.ShapeDtypeStruct((rows, D_S5), bf16),
        grid=(nblk,),
        in_specs=_s5_specs(idx) + [pl.BlockSpec((ROWS_BLK, D_S5), idx),
                                   _const_spec((1, D_S5)),
                                   _const_spec((D_S5, D_S5)),
                                   _const_spec((1, D_S5))],
        out_specs=pl.BlockSpec((ROWS_BLK, D_S5), idx),
        scratch_shapes=_S5_SCRATCH,
        compiler_params=_S5_PARAMS,
        name="s5_bwd_glu",
    )(u, w["wb"], w["a2"], w["wc"], w["wca"], w["wcb"], yf, d, w_glu, g_s5)


def _mixout_kernel(x_ref, s5n_ref, ug_ref, v_ref, wsp_ref, bsp_ref, ggm_ref, wout_ref,
                   gate_ref, o_ref):
    vb = v_ref[...]
    mixed = jnp.concatenate(
        [jnp.dot(wsp_ref[h], vb[:, h * GMLP_HEAD_DIM:(h + 1) * GMLP_HEAD_DIM],
                 preferred_element_type=f32) for h in range(GMLP_HEADS)], axis=-1)
    ygm = ug_ref[...] * (mixed + bsp_ref[...])
    gmn = _rms(ygm, ggm_ref[...]).astype(bf16)
    proj = (jnp.dot(s5n_ref[...], wout_ref[:D_S5, :], preferred_element_type=f32)
            + jnp.dot(gmn, wout_ref[D_S5:, :], preferred_element_type=f32))
    o_ref[...] = x_ref[...] + _per_batch(proj, gate_ref[...], jnp.multiply)


def _mixout(x, s5n, ug, v, wsp, bsp, g_gm, w_out, gate):
    rows = x.shape[0]
    full = pl.BlockSpec((ROWS_BLK, D_MODEL), lambda i: (i, 0))
    half = pl.BlockSpec((ROWS_BLK, D_S5), lambda i: (i, 0))
    return pl.pallas_call(
        _mixout_kernel,
        out_shape=jax.ShapeDtypeStruct((rows, D_MODEL), f32),
        grid=(rows // ROWS_BLK,),
        in_specs=[full, half, half, half,
                  _const_spec((GMLP_HEADS, ROWS_BLK, ROWS_BLK)),
                  _const_spec((ROWS_BLK, D_GMLP)),
                  _const_spec((1, D_GMLP)),
                  _const_spec((D_MODEL, D_MODEL)),
                  _const_spec((BATCH, D_MODEL))],
        out_specs=full,
        compiler_params=pltpu.CompilerParams(
            dimension_semantics=("parallel",), vmem_limit_bytes=VMEM_LIMIT),
        name="mixout",
    )(x, s5n, ug, v, wsp, bsp, g_gm, w_out, gate)


def _gmlp_spatial_weights(w_sp, b_sp):
    expand = jnp.repeat(jnp.eye(CHUNK, dtype=f32), BATCH, axis=0)
    big = jnp.einsum('rq,hqk,ck->hrc', expand, w_sp, expand)
    r = jnp.arange(ROWS_BLK) % BATCH
    same_batch = (r[:, None] == r[None, :]).astype(f32)
    wsp = (big * same_batch[None]).astype(bf16)
    bsp = jnp.broadcast_to(b_sp.T[:, None, :, None],
                           (CHUNK, BATCH, GMLP_HEADS, GMLP_HEAD_DIM)).reshape(ROWS_BLK, D_GMLP)
    return wsp, bsp


def _trunk(x, mod, p):
    bsz, seq, _ = x.shape
    assert bsz == BATCH and seq % T_BLK == 0 and CHUNK == T_BLK
    mod = mod.reshape(BATCH, N_MOD, D_MODEL).transpose(1, 0, 2)
    x1, u, ug, v = _ffn1_mixin(x, mod, p)
    yf = _s5_fwd(u, p["s5_f"])
    s5n = _s5_bwd_glu(u, yf, p["s5_b"], p["d"], p["w_glu"], p["g_s5"])
    x2 = _mixout(x1, s5n, ug, v, p["wsp"], p["bsp"], p["g_gm"], p["w_mix_out"], mod[5])
    return _ffn2_final(x2, mod[6:9], p)


def kernel(x_prompt, x_sample, c_prompt, c_sample, w_ada, b_ada, norm_ffn1_g, ffn1_w_in, ffn1_w_out, norm_mix_g, w_mix_in, s5_lam_re_f, s5_lam_im_f, s5_log_step_f, s5_b_re_f, s5_b_im_f, s5_c_re_f, s5_c_im_f, s5_lam_re_b, s5_lam_im_b, s5_log_step_b, s5_b_re_b, s5_b_im_b, s5_c_re_b, s5_c_im_b, s5_d, s5_w_glu, gmlp_ln_g, gmlp_ln_b, gmlp_w_sp, gmlp_b_sp, norm_out_s5_g, norm_out_gmlp_g, w_mix_out, norm_ffn2_g, ffn2_w_in, ffn2_w_out, final_norm_g):
    assert w_ada.shape[0] == 1, "single layer"
    row = lambda a: a.reshape(1, -1)
    both = lambda f, b: jnp.stack([f[0], b[0]])

    prep = _s5_prep(
        both(s5_lam_re_f, s5_lam_re_b), both(s5_lam_im_f, s5_lam_im_b),
        both(s5_log_step_f, s5_log_step_b), both(s5_b_re_f, s5_b_re_b),
        both(s5_b_im_f, s5_b_im_b), both(s5_c_re_f, s5_c_re_b), both(s5_c_im_f, s5_c_im_b))
    wsp, bsp = _gmlp_spatial_weights(gmlp_w_sp[0], gmlp_b_sp[0])

    p = dict(
        g1=row(norm_ffn1_g), ffn1_in=ffn1_w_in[0].astype(bf16), ffn1_out=ffn1_w_out[0].astype(bf16),
        g_mix=row(norm_mix_g), w_mix_in=w_mix_in[0].astype(bf16),
        ln_g=row(gmlp_ln_g), ln_b=row(gmlp_ln_b),
        s5_f=_s5_weights(prep, s5_c_re_f[0], s5_c_im_f[0], 0, reverse=False),
        s5_b=_s5_weights(prep, s5_c_re_b[0], s5_c_im_b[0], 1, reverse=True),
        d=row(s5_d), w_glu=s5_w_glu[0].astype(bf16), wsp=wsp, bsp=bsp,
        g_s5=row(norm_out_s5_g), g_gm=row(norm_out_gmlp_g), w_mix_out=w_mix_out[0].astype(bf16),
        g2=row(norm_ffn2_g), ffn2_in=ffn2_w_in[0].astype(bf16), ffn2_out=ffn2_w_out[0].astype(bf16),
        g_final=row(final_norm_g),
    )

    c = jnp.concatenate([c_prompt, c_sample], axis=0)
    mod = _ada(c, w_ada[0], b_ada)
    nb = c_prompt.shape[0]
    return (_trunk(x_prompt, mod[:nb], p), _trunk(x_sample, mod[nb:], p))
```

```python
import jax
import jax.numpy as jnp
from jax import lax
from jax.experimental import pallas as pl
from jax.experimental.pallas import tpu as pltpu

D_MODEL = 1024
BATCH = 8
D_S5 = 512
S5_GROUPS = 32
S5_GROUP = 16
S5_STATE = 64
N_STATE = S5_GROUPS * S5_STATE
D_GMLP = 512
GMLP_HEADS = 4
GMLP_HEAD_DIM = 128
CHUNK = 128
D_FF = 2816
N_MOD = 9
EPS = 1e-6

LANES = 128
PAIR_COLS = 2 * LANES
N_PAIRS = S5_GROUPS // 2
VMEM_LIMIT = 56 * 1024 * 1024

T_FFN = 64
TM_FFN = T_FFN * BATCH
FFN_SUBTILES = 2
TAIL_SUBTILES = 4
T_BLK = 128
ROWS_BLK = T_BLK * BATCH
HALF_ROWS = ROWS_BLK // 2
SCAN_COLS = 512

f32 = jnp.float32
bf16 = jnp.bfloat16


def _const_spec(shape):
    nd = len(shape)
    return pl.BlockSpec(shape, lambda i: (0,) * nd, pipeline_mode=pl.Buffered(1))


def _rms(x, g):
    ms = jnp.mean(x * x, axis=-1, keepdims=True)
    return x * lax.rsqrt(ms + EPS) * g


def _per_batch(x, v, op):
    rows, d = x.shape
    nb = v.shape[0]
    return op(x.reshape(nb, rows // nb, d), v[:, None, :]).reshape(rows, d)


def _modulate(h, shift, scale):
    return _per_batch(_per_batch(h, 1.0 + scale, jnp.multiply), shift, jnp.add)


def _to_time_major(x, nb):
    rows, d = x.shape
    return jnp.swapaxes(x.reshape(nb, rows // nb, d), 0, 1).reshape(rows, d)


def _to_batch_major(x, nb):
    rows, d = x.shape
    return jnp.swapaxes(x.reshape(rows // nb, nb, d), 0, 1)


def _ada_kernel(c_ref, w_ref, b_ref, o_ref):
    c = c_ref[...]
    s = c * jax.nn.sigmoid(c)
    o_ref[...] = jnp.dot(s.astype(bf16), w_ref[...].astype(bf16),
                         preferred_element_type=f32) + b_ref[...]


def _ada(c, w, b):
    n = c.shape[0]
    return pl.pallas_call(
        _ada_kernel,
        out_shape=jax.ShapeDtypeStruct((n, N_MOD * D_MODEL), f32),
        grid=(N_MOD,),
        in_specs=[pl.BlockSpec((n, D_MODEL), lambda j: (0, 0)),
                  pl.BlockSpec((D_MODEL, D_MODEL), lambda j: (0, j)),
                  pl.BlockSpec((1, D_MODEL), lambda j: (0, j))],
        out_specs=pl.BlockSpec((n, D_MODEL), lambda j: (0, j)),
        compiler_params=pltpu.CompilerParams(dimension_semantics=("arbitrary",)),
        name="ada",
    )(c, w, b)


def _swiglu_step(x, shift, scale, gate, g_ref, win_ref, wout_ref):
    h = _modulate(_rms(x, g_ref[...]), shift, scale).astype(bf16)
    a = jnp.dot(h, win_ref[...], preferred_element_type=f32)
    act = (jax.nn.silu(a[:, :D_FF]) * a[:, D_FF:]).astype(bf16)
    f = jnp.dot(act, wout_ref[...], preferred_element_type=f32)
    return x + _per_batch(f, 0.5 * gate, jnp.multiply)


def _ffn1_mixin_kernel(x_ref, mod1_ref, g1_ref, win_ref, wout_ref,
                       mod2_ref, g2_ref, wmix_ref, lng_ref, lnb_ref,
                       x1_ref, u_ref, ug_ref, v_ref):
    sub_t = T_FFN // FFN_SUBTILES
    sub_rows = sub_t * BATCH
    for s in range(FFN_SUBTILES):
        ts = slice(s * sub_t, (s + 1) * sub_t)
        slab = lambda a: a.reshape(BATCH, sub_t, a.shape[-1])
        x = x_ref[:, ts, :].reshape(sub_rows, D_MODEL)
        x1 = _swiglu_step(x, mod1_ref[0], mod1_ref[1], mod1_ref[2], g1_ref, win_ref, wout_ref)
        x1_ref[:, ts, :] = slab(x1)
        h = _modulate(_rms(x1, g2_ref[...]), mod2_ref[0], mod2_ref[1]).astype(bf16)
        z = jnp.dot(h, wmix_ref[...], preferred_element_type=f32)
        u_ref[s * sub_rows:(s + 1) * sub_rows, :] = _to_time_major(z[:, :D_S5], BATCH)
        zg = jax.nn.gelu(z[:, D_S5:])
        ug_ref[:, ts, :] = slab(zg[:, :D_GMLP])
        v = zg[:, D_GMLP:]
        mu = jnp.mean(v, axis=-1, keepdims=True)
        vc = v - mu
        var = jnp.mean(vc * vc, axis=-1, keepdims=True)
        v_ref[:, ts, :] = slab((vc * lax.rsqrt(var + EPS) * lng_ref[...] + lnb_ref[...]).astype(bf16))


def _ffn1_mixin(x, mod, p):
    seq = x.shape[1]
    rows = seq * BATCH
    slab_blk = lambda w: pl.BlockSpec((BATCH, T_FFN, w), lambda i: (0, i, 0))
    return pl.pallas_call(
        _ffn1_mixin_kernel,
        out_shape=(jax.ShapeDtypeStruct((BATCH, seq, D_MODEL), f32),
                   jax.ShapeDtypeStruct((rows, D_S5), f32),
                   jax.ShapeDtypeStruct((BATCH, seq, D_GMLP), f32),
                   jax.ShapeDtypeStruct((BATCH, seq, D_GMLP), bf16)),
        grid=(rows // TM_FFN,),
        in_specs=[slab_blk(D_MODEL),
                  _const_spec((3, BATCH, D_MODEL)),
                  _const_spec((1, D_MODEL)),
                  _const_spec((D_MODEL, 2 * D_FF)),
                  _const_spec((D_FF, D_MODEL)),
                  _const_spec((3, BATCH, D_MODEL)),
                  _const_spec((1, D_MODEL)),
                  _const_spec((D_MODEL, D_S5 + 2 * D_GMLP)),
                  _const_spec((1, D_GMLP)),
                  _const_spec((1, D_GMLP))],
        out_specs=(slab_blk(D_MODEL), pl.BlockSpec((TM_FFN, D_S5), lambda i: (i, 0)),
                   slab_blk(D_GMLP), slab_blk(D_GMLP)),
        compiler_params=pltpu.CompilerParams(
            dimension_semantics=("parallel",), vmem_limit_bytes=VMEM_LIMIT),
        name="ffn1_mixin",
    )(x, mod[0:3], p["g1"], p["ffn1_in"], p["ffn1_out"],
      mod[3:6], p["g_mix"], p["w_mix_in"], p["ln_g"], p["ln_b"])


def _s5_prep_kernel(lre_ref, lim_ref, ls_ref, bre_ref, bim_ref, cre_ref, cim_ref, seg_ref,
                    a2_ref, bb_ref, abb_ref, ca_ref, cb_ref):
    lre = lre_ref[...]
    lim = lim_ref[...]
    dt = jnp.exp(ls_ref[...])
    mag = jnp.exp(lre * dt)
    ab_re = mag * jnp.cos(lim * dt)
    ab_im = mag * jnp.sin(lim * dt)
    n_re = ab_re - 1.0
    n_im = ab_im
    den = lre * lre + lim * lim
    f_re = (n_re * lre + n_im * lim) / den
    f_im = (n_im * lre - n_re * lim) / den
    a2_ref[0] = ab_re * ab_re - ab_im * ab_im
    a2_ref[1] = 2.0 * ab_re * ab_im

    bre = bre_ref[...]
    bim = bim_ref[...]
    bb_re = f_re[:, None, :] * bre - f_im[:, None, :] * bim
    bb_im = f_re[:, None, :] * bim + f_im[:, None, :] * bre
    bb_ref[0] = bb_re
    bb_ref[1] = bb_im
    ar = ab_re[:, None, :]
    ai = ab_im[:, None, :]
    abb_ref[0] = ar * bb_re - ai * bb_im
    abb_ref[1] = ar * bb_im + ai * bb_re

    cre = cre_ref[...]
    cim = cim_ref[...]
    ca_ref[0] = cre * ar - cim * ai
    ca_ref[1] = cre * ai + cim * ar

    prod = (cre[:, :, None, :] * bb_re[:, None, :, :]
            - cim[:, :, None, :] * bb_im[:, None, :, :])
    prod = prod.reshape(2 * S5_GROUP * S5_GROUP, N_STATE)
    cb_ref[...] = jnp.dot(prod, seg_ref[...], precision=lax.Precision.HIGHEST,
                          preferred_element_type=f32)


def _s5_prep(lam_re, lam_im, log_step, b_re, b_im, c_re, c_im):
    flat = lambda a: a.reshape(2, N_STATE)
    ls = jnp.broadcast_to(log_step[:, :, None], (2, S5_GROUPS, S5_STATE))
    bt = lambda b: b.transpose(0, 3, 1, 2).reshape(2, S5_GROUP, N_STATE)
    ct = lambda c: c.transpose(0, 2, 1, 3).reshape(2, S5_GROUP, N_STATE)
    seg = jnp.repeat(jnp.eye(S5_GROUPS, dtype=f32), S5_STATE, axis=0)
    vec = jax.ShapeDtypeStruct((2, 2, N_STATE), f32)
    mat = jax.ShapeDtypeStruct((2, 2, S5_GROUP, N_STATE), f32)
    cb = jax.ShapeDtypeStruct((2 * S5_GROUP * S5_GROUP, S5_GROUPS), f32)
    return pl.pallas_call(
        _s5_prep_kernel, out_shape=(vec, mat, mat, mat, cb), name="s5_prep",
    )(flat(lam_re), flat(lam_im), flat(ls), bt(b_re), bt(b_im), ct(c_re), ct(c_im), seg)


def _s5_b_tiles(b_re, b_im):
    def gph(b):
        return b.reshape(S5_GROUP, N_PAIRS, 2, S5_STATE).transpose(1, 2, 0, 3)
    both = jnp.stack([gph(b_re), gph(b_im)], axis=3)
    pair = jnp.einsum('ijhcp,jk->ijhckp', both, jnp.eye(2, dtype=f32))
    pair = pair.reshape(N_PAIRS, 2 * S5_GROUP, PAIR_COLS)
    pairs_per_tile = LANES // (2 * S5_GROUP)
    slot = jax.nn.one_hot(jnp.arange(N_PAIRS) % pairs_per_tile, pairs_per_tile, dtype=f32)
    w = jnp.einsum('irn,is->isrn', pair, slot)
    return w.reshape(N_PAIRS, LANES, PAIR_COLS)


def _s5_c_tiles(c_re, c_im):
    half_groups = S5_GROUPS // 2
    eye = jnp.eye(half_groups, dtype=f32)

    def bd(c):
        c = c.reshape(S5_GROUP, 2, half_groups, S5_STATE)
        m = jnp.einsum('hagp,gk->agpkh', c, eye)
        return m.reshape(2, half_groups * S5_STATE, half_groups * S5_GROUP)
    return jnp.concatenate([bd(c_re), -bd(c_im)], axis=1).astype(bf16)


def _s5_feedthrough(cb):
    cb = cb.reshape(S5_GROUP, S5_GROUP, S5_GROUPS)
    m = jnp.einsum('oig,gk->giko', cb, jnp.eye(S5_GROUPS, dtype=f32))
    return m.reshape(D_S5, D_S5).astype(bf16)


def _s5_weights(prep, c_re, c_im, d, reverse):
    a2, bb, abb, ca, cb = prep
    plain = _s5_b_tiles(bb[0, d], bb[1, d])
    stepped = _s5_b_tiles(abb[0, d], abb[1, d])
    first, second = (plain, stepped) if reverse else (stepped, plain)
    ct = lambda c: c.transpose(1, 0, 2).reshape(S5_GROUP, N_STATE)
    return dict(
        wb=jnp.concatenate([first, second], axis=1).astype(bf16),
        a2=a2[:, d],
        wc=_s5_c_tiles(ct(c_re), ct(c_im)),
        wca=_s5_c_tiles(ca[0, d], ca[1, d]),
        wcb=_s5_feedthrough(cb[d * S5_GROUP * S5_GROUP:(d + 1) * S5_GROUP * S5_GROUP]),
    )


def _s5_outputs(u_ref, wb_ref, a2_ref, wc_ref, wca_ref, wcb_ref, xre_ref, xim_ref, reverse):
    carry_row = HALF_ROWS if reverse else 0
    own_row = 0 if reverse else BATCH
    prev_row = BATCH - own_row

    @pl.when(pl.program_id(0) == 0)
    def _():
        zeros = jnp.zeros((BATCH, N_STATE), f32)
        xre_ref[carry_row:carry_row + BATCH, :] = zeros
        xim_ref[carry_row:carry_row + BATCH, :] = zeros

    u4 = u_ref[...].reshape(T_BLK // 2, 2, BATCH, D_S5)
    u_even = u4[:, 0].reshape(HALF_ROWS, D_S5).astype(bf16)
    u_odd = u4[:, 1].reshape(HALF_ROWS, D_S5).astype(bf16)

    for i in range(N_PAIRS):
        lane0 = (i * 2 * S5_GROUP // LANES) * LANES
        lhs = jnp.concatenate([u_even[:, lane0:lane0 + LANES], u_odd[:, lane0:lane0 + LANES]], axis=1)
        r = jnp.dot(lhs, wb_ref[i], preferred_element_type=f32)
        xre_ref[own_row:own_row + HALF_ROWS, i * LANES:(i + 1) * LANES] = r[:, :LANES]
        xim_ref[own_row:own_row + HALF_ROWS, i * LANES:(i + 1) * LANES] = r[:, LANES:]

    n_steps = T_BLK // 2
    for cb in range(N_STATE // SCAN_COLS):
        cols = pl.ds(cb * SCAN_COLS, SCAN_COLS)
        ar = jnp.broadcast_to(a2_ref[0:1, cols], (BATCH, SCAN_COLS))
        ai = jnp.broadcast_to(a2_ref[1:2, cols], (BATCH, SCAN_COLS))

        def step(k, carry):
            sr, si = carry
            kk = (n_steps - 1 - k) if reverse else k
            rows = pl.ds(pl.multiple_of(own_row + kk * BATCH, BATCH), BATCH)
            nr = ar * sr - ai * si + xre_ref[rows, cols]
            ni = ar * si + ai * sr + xim_ref[rows, cols]
            xre_ref[rows, cols] = nr
            xim_ref[rows, cols] = ni
            return nr, ni

        init = (xre_ref[carry_row:carry_row + BATCH, cols], xim_ref[carry_row:carry_row + BATCH, cols])
        lax.fori_loop(0, n_steps, step, init, unroll=True)

    half = N_STATE // 2
    width = D_S5 // 2
    u_skip = u_odd if reverse else u_even
    y_own, y_skip = [], []
    for hf in range(2):
        def proj(row0, w_ref):
            xr = xre_ref[row0:row0 + HALF_ROWS, hf * half:(hf + 1) * half].astype(bf16)
            xi = xim_ref[row0:row0 + HALF_ROWS, hf * half:(hf + 1) * half].astype(bf16)
            return (jnp.dot(xr, w_ref[hf, :half, :], preferred_element_type=f32)
                    + jnp.dot(xi, w_ref[hf, half:, :], preferred_element_type=f32))
        y_own.append(proj(own_row, wc_ref))
        y_skip.append(proj(prev_row, wca_ref)
                      + jnp.dot(u_skip, wcb_ref[:, hf * width:(hf + 1) * width],
                                preferred_element_type=f32))
    y_own = jnp.concatenate(y_own, axis=1).reshape(T_BLK // 2, 1, BATCH, D_S5)
    y_skip = jnp.concatenate(y_skip, axis=1).reshape(T_BLK // 2, 1, BATCH, D_S5)
    pair = (y_own, y_skip) if reverse else (y_skip, y_own)
    y = jnp.concatenate(pair, axis=1).reshape(ROWS_BLK, D_S5)

    last_row = 0 if reverse else HALF_ROWS
    xre_ref[carry_row:carry_row + BATCH, :] = xre_ref[last_row:last_row + BATCH, :]
    xim_ref[carry_row:carry_row + BATCH, :] = xim_ref[last_row:last_row + BATCH, :]
    return y


def _s5_fwd_kernel(u_ref, wb_ref, a2_ref, wc_ref, wca_ref, wcb_ref, y_ref, xre_ref, xim_ref):
    y_ref[...] = _s5_outputs(u_ref, wb_ref, a2_ref, wc_ref, wca_ref, wcb_ref,
                             xre_ref, xim_ref, reverse=False)


def _s5_bwd_kernel(u_ref, wb_ref, a2_ref, wc_ref, wca_ref, wcb_ref, yf_ref, d_ref, wglu_ref,
                   g_ref, o_ref, xre_ref, xim_ref):
    yb = _s5_outputs(u_ref, wb_ref, a2_ref, wc_ref, wca_ref, wcb_ref,
                     xre_ref, xim_ref, reverse=True)
    y = jax.nn.gelu(yf_ref[...] + yb + d_ref[...] * u_ref[...])
    y = y * jax.nn.sigmoid(jnp.dot(y.astype(bf16), wglu_ref[...], preferred_element_type=f32))
    o_ref[...] = _to_batch_major(_rms(y, g_ref[...]), BATCH).astype(bf16)


def _s5_specs(idx):
    return [pl.BlockSpec((ROWS_BLK, D_S5), idx),
            _const_spec((N_PAIRS, PAIR_COLS, PAIR_COLS)),
            _const_spec((2, N_STATE)),
            _const_spec((2, N_STATE, D_S5 // 2)),
            _const_spec((2, N_STATE, D_S5 // 2)),
            _const_spec((D_S5, D_S5))]


_S5_SCRATCH = [pltpu.VMEM((HALF_ROWS + BATCH, N_STATE), f32),
               pltpu.VMEM((HALF_ROWS + BATCH, N_STATE), f32)]
_S5_PARAMS = pltpu.CompilerParams(dimension_semantics=("arbitrary",), vmem_limit_bytes=VMEM_LIMIT)


def _s5_fwd(u, w):
    rows = u.shape[0]
    idx = lambda i: (i, 0)
    return pl.pallas_call(
        _s5_fwd_kernel,
        out_shape=jax.ShapeDtypeStruct((rows, D_S5), f32),
        grid=(rows // ROWS_BLK,),
        in_specs=_s5_specs(idx),
        out_specs=pl.BlockSpec((ROWS_BLK, D_S5), idx),
        scratch_shapes=_S5_SCRATCH,
        compiler_params=_S5_PARAMS,
        name="s5_fwd",
    )(u, w["wb"], w["a2"], w["wc"], w["wca"], w["wcb"])


def _s5_bwd_glu(u, yf, w, d, w_glu, g_s5):
    rows = u.shape[0]
    nblk = rows // ROWS_BLK
    idx = lambda i: (nblk - 1 - i, 0)
    return pl.pallas_call(
        _s5_bwd_kernel,
        out_shape=jax.ShapeDtypeStruct((BATCH, rows // BATCH, D_S5), bf16),
        grid=(nblk,),
        in_specs=_s5_specs(idx) + [pl.BlockSpec((ROWS_BLK, D_S5), idx),
                                   _const_spec((1, D_S5)),
                                   _const_spec((D_S5, D_S5)),
                                   _const_spec((1, D_S5))],
        out_specs=pl.BlockSpec((BATCH, T_BLK, D_S5), lambda i: (0, nblk - 1 - i, 0)),
        scratch_shapes=_S5_SCRATCH,
        compiler_params=_S5_PARAMS,
        name="s5_bwd_glu",
    )(u, w["wb"], w["a2"], w["wc"], w["wca"], w["wcb"], yf, d, w_glu, g_s5)


def _tail_kernel(x1_ref, s5n_ref, ug_ref, v_ref, wsp_ref, bsp_ref, ggm_ref, wmo_ref, gate_ref,
                 mod_ref, g2_ref, win_ref, wout_ref, gf_ref, o_ref):
    mixed = []
    for h in range(GMLP_HEADS):
        hc = slice(h * GMLP_HEAD_DIM, (h + 1) * GMLP_HEAD_DIM)
        rhs = jnp.concatenate([v_ref[b, :, hc] for b in range(BATCH)], axis=1)
        mixed.append(jnp.dot(wsp_ref[h], rhs, preferred_element_type=f32))

    nb = BATCH // TAIL_SUBTILES
    for s in range(TAIL_SUBTILES):
        bs = slice(s * nb, (s + 1) * nb)
        flat = lambda a: a.reshape(nb * CHUNK, a.shape[-1])
        gm = []
        for b in range(s * nb, (s + 1) * nb):
            lanes = slice(b * GMLP_HEAD_DIM, (b + 1) * GMLP_HEAD_DIM)
            gm.append(jnp.concatenate([m[:, lanes] for m in mixed], axis=1))
        gm = jnp.concatenate(gm, axis=0).reshape(nb, CHUNK, D_GMLP) + bsp_ref[...][None]
        ygm = flat(ug_ref[bs] * gm)
        gmn = _rms(ygm, ggm_ref[...]).astype(bf16)
        proj = (jnp.dot(flat(s5n_ref[bs]), wmo_ref[:D_S5, :], preferred_element_type=f32)
                + jnp.dot(gmn, wmo_ref[D_S5:, :], preferred_element_type=f32))
        x2 = flat(x1_ref[bs]) + _per_batch(proj, gate_ref[bs], jnp.multiply)
        y = _swiglu_step(x2, mod_ref[0, bs], mod_ref[1, bs], mod_ref[2, bs], g2_ref, win_ref, wout_ref)
        o_ref[bs] = _rms(y, gf_ref[...]).reshape(nb, CHUNK, D_MODEL)


def _tail(x1, s5n, ug, v, mod, p):
    seq = x1.shape[1]
    blk = lambda w: pl.BlockSpec((BATCH, CHUNK, w), lambda i: (0, i, 0))
    return pl.pallas_call(
        _tail_kernel,
        out_shape=jax.ShapeDtypeStruct((BATCH, seq, D_MODEL), f32),
        grid=(seq // CHUNK,),
        in_specs=[blk(D_MODEL), blk(D_S5), blk(D_GMLP), blk(D_GMLP),
                  _const_spec((GMLP_HEADS, CHUNK, CHUNK)),
                  _const_spec((CHUNK, D_GMLP)),
                  _const_spec((1, D_GMLP)),
                  _const_spec((D_MODEL, D_MODEL)),
                  _const_spec((BATCH, D_MODEL)),
                  _const_spec((3, BATCH, D_MODEL)),
                  _const_spec((1, D_MODEL)),
                  _const_spec((D_MODEL, 2 * D_FF)),
                  _const_spec((D_FF, D_MODEL)),
                  _const_spec((1, D_MODEL))],
        out_specs=blk(D_MODEL),
        compiler_params=pltpu.CompilerParams(
            dimension_semantics=("parallel",), vmem_limit_bytes=VMEM_LIMIT),
        name="tail",
    )(x1, s5n, ug, v, p["wsp"], p["bsp"], p["g_gm"], p["w_mix_out"], mod[5],
      mod[6:9], p["g2"], p["ffn2_in"], p["ffn2_out"], p["g_final"])


def _trunk(x, mod, p):
    bsz, seq, _ = x.shape
    assert bsz == BATCH and seq % T_BLK == 0 and CHUNK == T_BLK
    mod = mod.reshape(BATCH, N_MOD, D_MODEL).transpose(1, 0, 2)
    x1, u, ug, v = _ffn1_mixin(x, mod, p)
    yf = _s5_fwd(u, p["s5_f"])
    s5n = _s5_bwd_glu(u, yf, p["s5_b"], p["d"], p["w_glu"], p["g_s5"])
    return _tail(x1, s5n, ug, v, mod, p)


def kernel(x_prompt, x_sample, c_prompt, c_sample, w_ada, b_ada, norm_ffn1_g, ffn1_w_in, ffn1_w_out, norm_mix_g, w_mix_in, s5_lam_re_f, s5_lam_im_f, s5_log_step_f, s5_b_re_f, s5_b_im_f, s5_c_re_f, s5_c_im_f, s5_lam_re_b, s5_lam_im_b, s5_log_step_b, s5_b_re_b, s5_b_im_b, s5_c_re_b, s5_c_im_b, s5_d, s5_w_glu, gmlp_ln_g, gmlp_ln_b, gmlp_w_sp, gmlp_b_sp, norm_out_s5_g, norm_out_gmlp_g, w_mix_out, norm_ffn2_g, ffn2_w_in, ffn2_w_out, final_norm_g):
    assert w_ada.shape[0] == 1, "single layer"
    row = lambda a: a.reshape(1, -1)
    both = lambda f, b: jnp.stack([f[0], b[0]])

    prep = _s5_prep(
        both(s5_lam_re_f, s5_lam_re_b), both(s5_lam_im_f, s5_lam_im_b),
        both(s5_log_step_f, s5_log_step_b), both(s5_b_re_f, s5_b_re_b),
        both(s5_b_im_f, s5_b_im_b), both(s5_c_re_f, s5_c_re_b), both(s5_c_im_f, s5_c_im_b))
    bsp = jnp.repeat(gmlp_b_sp[0].T, GMLP_HEAD_DIM, axis=1)

    p = dict(
        g1=row(norm_ffn1_g), ffn1_in=ffn1_w_in[0].astype(bf16), ffn1_out=ffn1_w_out[0].astype(bf16),
        g_mix=row(norm_mix_g), w_mix_in=w_mix_in[0].astype(bf16),
        ln_g=row(gmlp_ln_g), ln_b=row(gmlp_ln_b),
        s5_f=_s5_weights(prep, s5_c_re_f[0], s5_c_im_f[0], 0, reverse=False),
        s5_b=_s5_weights(prep, s5_c_re_b[0], s5_c_im_b[0], 1, reverse=True),
        d=row(s5_d), w_glu=s5_w_glu[0].astype(bf16), wsp=gmlp_w_sp[0].astype(bf16), bsp=bsp,
        g_s5=row(norm_out_s5_g), g_gm=row(norm_out_gmlp_g), w_mix_out=w_mix_out[0].astype(bf16),
        g2=row(norm_ffn2_g), ffn2_in=ffn2_w_in[0].astype(bf16), ffn2_out=ffn2_w_out[0].astype(bf16),
        g_final=row(final_norm_g),
    )

    c = jnp.concatenate([c_prompt, c_sample], axis=0)
    mod = _ada(c, w_ada[0], b_ada)
    nb = c_prompt.shape[0]
    return (_trunk(x_prompt, mod[:nb], p), _trunk(x_sample, mod[nb:], p))
```

```python
import jax
import jax.numpy as jnp
from jax import lax
from jax.experimental import pallas as pl
from jax.experimental.pallas import tpu as pltpu

D_MODEL = 1024
BATCH = 8
D_S5 = 512
S5_GROUPS = 32
S5_GROUP = 16
S5_STATE = 64
N_STATE = S5_GROUPS * S5_STATE
D_GMLP = 512
GMLP_HEADS = 4
GMLP_HEAD_DIM = 128
CHUNK = 128
D_FF = 2816
N_MOD = 9
EPS = 1e-6

LANES = 128
PAIR_COLS = 2 * LANES
N_PAIRS = S5_GROUPS // 2
VMEM_LIMIT = 56 * 1024 * 1024

T_FFN = 128
TM_FFN = T_FFN * BATCH
FFN_SUBTILES = 4
TAIL_SUBTILES = 4
T_BLK = 128
ROWS_BLK = T_BLK * BATCH
HALF_ROWS = ROWS_BLK // 2
SCAN_COLS = 512

f32 = jnp.float32
bf16 = jnp.bfloat16


def _const_spec(shape):
    nd = len(shape)
    return pl.BlockSpec(shape, lambda i: (0,) * nd, pipeline_mode=pl.Buffered(1))


def _rms(x, g):
    ms = jnp.mean(x * x, axis=-1, keepdims=True)
    return x * lax.rsqrt(ms + EPS) * g


def _per_batch(x, v, op):
    rows, d = x.shape
    nb = v.shape[0]
    return op(x.reshape(nb, rows // nb, d), v[:, None, :]).reshape(rows, d)


def _modulate(h, shift, scale):
    return _per_batch(_per_batch(h, 1.0 + scale, jnp.multiply), shift, jnp.add)


def _to_time_major(x, nb):
    rows, d = x.shape
    return jnp.swapaxes(x.reshape(nb, rows // nb, d), 0, 1).reshape(rows, d)


def _to_batch_major(x, nb):
    rows, d = x.shape
    return jnp.swapaxes(x.reshape(rows // nb, nb, d), 0, 1)


def _ada_kernel(c_ref, w_ref, b_ref, o_ref):
    c = c_ref[...]
    s = c * jax.nn.sigmoid(c)
    o_ref[...] = jnp.dot(s.astype(bf16), w_ref[...].astype(bf16),
                         preferred_element_type=f32) + b_ref[...]


def _ada(c, w, b):
    n = c.shape[0]
    return pl.pallas_call(
        _ada_kernel,
        out_shape=jax.ShapeDtypeStruct((n, N_MOD * D_MODEL), f32),
        grid=(N_MOD,),
        in_specs=[pl.BlockSpec((n, D_MODEL), lambda j: (0, 0)),
                  pl.BlockSpec((D_MODEL, D_MODEL), lambda j: (0, j)),
                  pl.BlockSpec((1, D_MODEL), lambda j: (0, j))],
        out_specs=pl.BlockSpec((n, D_MODEL), lambda j: (0, j)),
        compiler_params=pltpu.CompilerParams(dimension_semantics=("arbitrary",)),
        name="ada",
    )(c, w, b)


def _swiglu_step(x, shift, scale, gate, g_ref, win_ref, wout_ref):
    h = _modulate(_rms(x, g_ref[...]), shift, scale).astype(bf16)
    a = jnp.dot(h, win_ref[...], preferred_element_type=f32)
    act = (jax.nn.silu(a[:, :D_FF]) * a[:, D_FF:]).astype(bf16)
    f = jnp.dot(act, wout_ref[...], preferred_element_type=f32)
    return x + _per_batch(f, 0.5 * gate, jnp.multiply)


def _ffn1_mixin_kernel(x_ref, mod1_ref, g1_ref, win_ref, wout_ref,
                       mod2_ref, g2_ref, wmix_ref, lng_ref, lnb_ref,
                       x1_ref, u_ref, ug_ref, v_ref):
    sub_t = T_FFN // FFN_SUBTILES
    sub_rows = sub_t * BATCH
    for s in range(FFN_SUBTILES):
        ts = slice(s * sub_t, (s + 1) * sub_t)
        slab = lambda a: a.reshape(BATCH, sub_t, a.shape[-1])
        x = x_ref[:, ts, :].reshape(sub_rows, D_MODEL)
        x1 = _swiglu_step(x, mod1_ref[0], mod1_ref[1], mod1_ref[2], g1_ref, win_ref, wout_ref)
        x1_ref[:, ts, :] = slab(x1)
        h = _modulate(_rms(x1, g2_ref[...]), mod2_ref[0], mod2_ref[1]).astype(bf16)
        z = jnp.dot(h, wmix_ref[...], preferred_element_type=f32)
        u_ref[s * sub_rows:(s + 1) * sub_rows, :] = _to_time_major(z[:, :D_S5], BATCH)
        zg = jax.nn.gelu(z[:, D_S5:])
        ug_ref[:, ts, :] = slab(zg[:, :D_GMLP])
        v = zg[:, D_GMLP:]
        mu = jnp.mean(v, axis=-1, keepdims=True)
        vc = v - mu
        var = jnp.mean(vc * vc, axis=-1, keepdims=True)
        v_ref[:, ts, :] = slab((vc * lax.rsqrt(var + EPS) * lng_ref[...] + lnb_ref[...]).astype(bf16))


def _ffn1_mixin(x, mod, p):
    seq = x.shape[1]
    rows = seq * BATCH
    slab_blk = lambda w: pl.BlockSpec((BATCH, T_FFN, w), lambda i: (0, i, 0))
    return pl.pallas_call(
        _ffn1_mixin_kernel,
        out_shape=(jax.ShapeDtypeStruct((BATCH, seq, D_MODEL), f32),
                   jax.ShapeDtypeStruct((rows, D_S5), f32),
                   jax.ShapeDtypeStruct((BATCH, seq, D_GMLP), f32),
                   jax.ShapeDtypeStruct((BATCH, seq, D_GMLP), bf16)),
        grid=(rows // TM_FFN,),
        in_specs=[slab_blk(D_MODEL),
                  _const_spec((3, BATCH, D_MODEL)),
                  _const_spec((1, D_MODEL)),
                  _const_spec((D_MODEL, 2 * D_FF)),
                  _const_spec((D_FF, D_MODEL)),
                  _const_spec((3, BATCH, D_MODEL)),
                  _const_spec((1, D_MODEL)),
                  _const_spec((D_MODEL, D_S5 + 2 * D_GMLP)),
                  _const_spec((1, D_GMLP)),
                  _const_spec((1, D_GMLP))],
        out_specs=(slab_blk(D_MODEL), pl.BlockSpec((TM_FFN, D_S5), lambda i: (i, 0)),
                   slab_blk(D_GMLP), slab_blk(D_GMLP)),
        compiler_params=pltpu.CompilerParams(
            dimension_semantics=("parallel",), vmem_limit_bytes=VMEM_LIMIT),
        name="ffn1_mixin",
    )(x, mod[0:3], p["g1"], p["ffn1_in"], p["ffn1_out"],
      mod[3:6], p["g_mix"], p["w_mix_in"], p["ln_g"], p["ln_b"])


def _s5_prep_kernel(lre_ref, lim_ref, ls_ref, bre_ref, bim_ref, cre_ref, cim_ref, seg_ref,
                    a2_ref, bb_ref, abb_ref, ca_ref, cb_ref):
    lre = lre_ref[...]
    lim = lim_ref[...]
    dt = jnp.exp(ls_ref[...])
    mag = jnp.exp(lre * dt)
    ab_re = mag * jnp.cos(lim * dt)
    ab_im = mag * jnp.sin(lim * dt)
    n_re = ab_re - 1.0
    n_im = ab_im
    den = lre * lre + lim * lim
    f_re = (n_re * lre + n_im * lim) / den
    f_im = (n_im * lre - n_re * lim) / den
    a2_ref[0] = ab_re * ab_re - ab_im * ab_im
    a2_ref[1] = 2.0 * ab_re * ab_im

    bre = bre_ref[...]
    bim = bim_ref[...]
    bb_re = f_re[:, None, :] * bre - f_im[:, None, :] * bim
    bb_im = f_re[:, None, :] * bim + f_im[:, None, :] * bre
    bb_ref[0] = bb_re
    bb_ref[1] = bb_im
    ar = ab_re[:, None, :]
    ai = ab_im[:, None, :]
    abb_ref[0] = ar * bb_re - ai * bb_im
    abb_ref[1] = ar * bb_im + ai * bb_re

    cre = cre_ref[...]
    cim = cim_ref[...]
    ca_ref[0] = cre * ar - cim * ai
    ca_ref[1] = cre * ai + cim * ar

    prod = (cre[:, :, None, :] * bb_re[:, None, :, :]
            - cim[:, :, None, :] * bb_im[:, None, :, :])
    prod = prod.reshape(2 * S5_GROUP * S5_GROUP, N_STATE)
    cb_ref[...] = jnp.dot(prod, seg_ref[...], precision=lax.Precision.HIGHEST,
                          preferred_element_type=f32)


def _s5_prep(lam_re, lam_im, log_step, b_re, b_im, c_re, c_im):
    flat = lambda a: a.reshape(2, N_STATE)
    ls = jnp.broadcast_to(log_step[:, :, None], (2, S5_GROUPS, S5_STATE))
    bt = lambda b: b.transpose(0, 3, 1, 2).reshape(2, S5_GROUP, N_STATE)
    ct = lambda c: c.transpose(0, 2, 1, 3).reshape(2, S5_GROUP, N_STATE)
    seg = jnp.repeat(jnp.eye(S5_GROUPS, dtype=f32), S5_STATE, axis=0)
    vec = jax.ShapeDtypeStruct((2, 2, N_STATE), f32)
    mat = jax.ShapeDtypeStruct((2, 2, S5_GROUP, N_STATE), f32)
    cb = jax.ShapeDtypeStruct((2 * S5_GROUP * S5_GROUP, S5_GROUPS), f32)
    return pl.pallas_call(
        _s5_prep_kernel, out_shape=(vec, mat, mat, mat, cb), name="s5_prep",
    )(flat(lam_re), flat(lam_im), flat(ls), bt(b_re), bt(b_im), ct(c_re), ct(c_im), seg)


def _s5_b_tiles(b_re, b_im):
    def gph(b):
        return b.reshape(S5_GROUP, N_PAIRS, 2, S5_STATE).transpose(1, 2, 0, 3)
    both = jnp.stack([gph(b_re), gph(b_im)], axis=3)
    pair = jnp.einsum('ijhcp,jk->ijhckp', both, jnp.eye(2, dtype=f32))
    pair = pair.reshape(N_PAIRS, 2 * S5_GROUP, PAIR_COLS)
    pairs_per_tile = LANES // (2 * S5_GROUP)
    slot = jax.nn.one_hot(jnp.arange(N_PAIRS) % pairs_per_tile, pairs_per_tile, dtype=f32)
    w = jnp.einsum('irn,is->isrn', pair, slot)
    return w.reshape(N_PAIRS, LANES, PAIR_COLS)


def _s5_c_tiles(c_re, c_im):
    half_groups = S5_GROUPS // 2
    eye = jnp.eye(half_groups, dtype=f32)

    def bd(c):
        c = c.reshape(S5_GROUP, 2, half_groups, S5_STATE)
        m = jnp.einsum('hagp,gk->agpkh', c, eye)
        return m.reshape(2, half_groups * S5_STATE, half_groups * S5_GROUP)
    return jnp.concatenate([bd(c_re), -bd(c_im)], axis=1).astype(bf16)


def _s5_feedthrough(cb):
    cb = cb.reshape(S5_GROUP, S5_GROUP, S5_GROUPS)
    m = jnp.einsum('oig,gk->giko', cb, jnp.eye(S5_GROUPS, dtype=f32))
    return m.reshape(D_S5, D_S5).astype(bf16)


def _s5_weights(prep, c_re, c_im, d, reverse):
    a2, bb, abb, ca, cb = prep
    plain = _s5_b_tiles(bb[0, d], bb[1, d])
    stepped = _s5_b_tiles(abb[0, d], abb[1, d])
    first, second = (plain, stepped) if reverse else (stepped, plain)
    ct = lambda c: c.transpose(1, 0, 2).reshape(S5_GROUP, N_STATE)
    return dict(
        wb=jnp.concatenate([first, second], axis=1).astype(bf16),
        a2=a2[:, d],
        wcat=jnp.concatenate([_s5_c_tiles(ca[0, d], ca[1, d]), _s5_c_tiles(ct(c_re), ct(c_im))], axis=2),
        wcb=_s5_feedthrough(cb[d * S5_GROUP * S5_GROUP:(d + 1) * S5_GROUP * S5_GROUP]),
    )


def _s5_outputs(u_ref, wb_ref, a2_ref, wcat_ref, wcb_ref, w_re_ref, w_im_ref, xs_re_ref, xs_im_ref,
                st_ref, reverse):
    n_steps = T_BLK // 2
    tile = lambda j: slice(j * BATCH, (j + 1) * BATCH)

    @pl.when(pl.program_id(0) == 0)
    def _():
        st_ref[...] = jnp.zeros_like(st_ref)

    u4 = u_ref[...].reshape(n_steps, 2, BATCH, D_S5)
    u_even = u4[:, 0].reshape(HALF_ROWS, D_S5).astype(bf16)
    u_odd = u4[:, 1].reshape(HALF_ROWS, D_S5).astype(bf16)

    for i in range(N_PAIRS):
        lane0 = (i * 2 * S5_GROUP // LANES) * LANES
        lhs = jnp.concatenate([u_even[:, lane0:lane0 + LANES], u_odd[:, lane0:lane0 + LANES]], axis=1)
        r = jnp.dot(lhs, wb_ref[i], preferred_element_type=f32)
        w_re_ref[:, i * LANES:(i + 1) * LANES] = r[:, :LANES]
        w_im_ref[:, i * LANES:(i + 1) * LANES] = r[:, LANES:]

    zero_tile = jnp.zeros((BATCH, SCAN_COLS), f32)
    for cb in range(N_STATE // SCAN_COLS):
        cols = slice(cb * SCAN_COLS, (cb + 1) * SCAN_COLS)
        ar = jnp.broadcast_to(a2_ref[0:1, cols], (BATCH, SCAN_COLS))
        ai = jnp.broadcast_to(a2_ref[1:2, cols], (BATCH, SCAN_COLS))

        def put_pair(j, lo, hi):
            rows = slice(j * BATCH, (j + 2) * BATCH)
            xs_re_ref[rows, cols] = jnp.concatenate([lo[0], hi[0]], axis=0).astype(bf16)
            xs_im_ref[rows, cols] = jnp.concatenate([lo[1], hi[1]], axis=0).astype(bf16)

        state = (st_ref[0, :, cols], st_ref[1, :, cols])
        pad = (zero_tile, zero_tile)
        held = None
        if reverse:
            put_pair(n_steps, state, pad)
        else:
            held = state
        for k in range(n_steps):
            kk = (n_steps - 1 - k) if reverse else k
            sr, si = state
            state = (ar * sr - ai * si + w_re_ref[tile(kk), cols],
                     ar * si + ai * sr + w_im_ref[tile(kk), cols])
            j = kk if reverse else kk + 1
            if held is None:
                held = state
            elif reverse:
                put_pair(j, state, held)
                held = None
            else:
                put_pair(j - 1, held, state)
                held = None
        if not reverse:
            put_pair(n_steps, held, pad)
        st_ref[0, :, cols] = state[0]
        st_ref[1, :, cols] = state[1]

    half = N_STATE // 2
    width = D_S5 // 2
    u_skip = u_odd if reverse else u_even
    own0, skip0 = (0, BATCH) if reverse else (BATCH, 0)
    y_own, y_skip = [], []
    for hf in range(2):
        hc = slice(hf * half, (hf + 1) * half)
        lhs = jnp.concatenate([xs_re_ref[:, hc], xs_im_ref[:, hc]], axis=1)
        r = jnp.dot(lhs, wcat_ref[hf], preferred_element_type=f32)
        y_own.append(r[own0:own0 + HALF_ROWS, width:])
        y_skip.append(r[skip0:skip0 + HALF_ROWS, :width]
                      + jnp.dot(u_skip, wcb_ref[:, hf * width:(hf + 1) * width],
                                preferred_element_type=f32))
    y_own = jnp.concatenate(y_own, axis=1).reshape(n_steps, 1, BATCH, D_S5)
    y_skip = jnp.concatenate(y_skip, axis=1).reshape(n_steps, 1, BATCH, D_S5)
    pair = (y_own, y_skip) if reverse else (y_skip, y_own)
    return jnp.concatenate(pair, axis=1).reshape(ROWS_BLK, D_S5)


def _s5_fwd_kernel(u_ref, wb_ref, a2_ref, wcat_ref, wcb_ref, y_ref, *scratch):
    y_ref[...] = _s5_outputs(u_ref, wb_ref, a2_ref, wcat_ref, wcb_ref, *scratch, reverse=False)


def _s5_bwd_kernel(u_ref, wb_ref, a2_ref, wcat_ref, wcb_ref, yf_ref, d_ref, wglu_ref,
                   g_ref, o_ref, *scratch):
    yb = _s5_outputs(u_ref, wb_ref, a2_ref, wcat_ref, wcb_ref, *scratch, reverse=True)
    y = jax.nn.gelu(yf_ref[...] + yb + d_ref[...] * u_ref[...])
    y = y * jax.nn.sigmoid(jnp.dot(y.astype(bf16), wglu_ref[...], preferred_element_type=f32))
    o_ref[...] = _to_batch_major(_rms(y, g_ref[...]), BATCH).astype(bf16)


def _s5_specs(idx):
    return [pl.BlockSpec((ROWS_BLK, D_S5), idx),
            _const_spec((N_PAIRS, PAIR_COLS, PAIR_COLS)),
            _const_spec((2, N_STATE)),
            _const_spec((2, N_STATE, D_S5)),
            _const_spec((D_S5, D_S5))]


STATE_ROWS = HALF_ROWS + 2 * BATCH
_S5_SCRATCH = [pltpu.VMEM((HALF_ROWS, N_STATE), f32),
               pltpu.VMEM((HALF_ROWS, N_STATE), f32),
               pltpu.VMEM((STATE_ROWS, N_STATE), bf16),
               pltpu.VMEM((STATE_ROWS, N_STATE), bf16),
               pltpu.VMEM((2, BATCH, N_STATE), f32)]
_S5_PARAMS = pltpu.CompilerParams(dimension_semantics=("arbitrary",), vmem_limit_bytes=VMEM_LIMIT)


def _s5_fwd(u, w):
    rows = u.shape[0]
    idx = lambda i: (i, 0)
    return pl.pallas_call(
        _s5_fwd_kernel,
        out_shape=jax.ShapeDtypeStruct((rows, D_S5), f32),
        grid=(rows // ROWS_BLK,),
        in_specs=_s5_specs(idx),
        out_specs=pl.BlockSpec((ROWS_BLK, D_S5), idx),
        scratch_shapes=_S5_SCRATCH,
        compiler_params=_S5_PARAMS,
        name="s5_fwd",
    )(u, w["wb"], w["a2"], w["wcat"], w["wcb"])


def _s5_bwd_glu(u, yf, w, d, w_glu, g_s5):
    rows = u.shape[0]
    nblk = rows // ROWS_BLK
    idx = lambda i: (nblk - 1 - i, 0)
    return pl.pallas_call(
        _s5_bwd_kernel,
        out_shape=jax.ShapeDtypeStruct((BATCH, rows // BATCH, D_S5), bf16),
        grid=(nblk,),
        in_specs=_s5_specs(idx) + [pl.BlockSpec((ROWS_BLK, D_S5), idx),
                                   _const_spec((1, D_S5)),
                                   _const_spec((D_S5, D_S5)),
                                   _const_spec((1, D_S5))],
        out_specs=pl.BlockSpec((BATCH, T_BLK, D_S5), lambda i: (0, nblk - 1 - i, 0)),
        scratch_shapes=_S5_SCRATCH,
        compiler_params=_S5_PARAMS,
        name="s5_bwd_glu",
    )(u, w["wb"], w["a2"], w["wcat"], w["wcb"], yf, d, w_glu, g_s5)


def _tail_kernel(x1_ref, s5n_ref, ug_ref, v_ref, wsp_ref, bsp_ref, ggm_ref, wmo_ref, gate_ref,
                 mod_ref, g2_ref, win_ref, wout_ref, gf_ref, o_ref):
    mixed = []
    for h in range(GMLP_HEADS):
        hc = slice(h * GMLP_HEAD_DIM, (h + 1) * GMLP_HEAD_DIM)
        rhs = jnp.concatenate([v_ref[b, :, hc] for b in range(BATCH)], axis=1)
        mixed.append(jnp.dot(wsp_ref[h], rhs, preferred_element_type=f32))

    nb = BATCH // TAIL_SUBTILES
    for s in range(TAIL_SUBTILES):
        bs = slice(s * nb, (s + 1) * nb)
        flat = lambda a: a.reshape(nb * CHUNK, a.shape[-1])
        gm = []
        for b in range(s * nb, (s + 1) * nb):
            lanes = slice(b * GMLP_HEAD_DIM, (b + 1) * GMLP_HEAD_DIM)
            gm.append(jnp.concatenate([m[:, lanes] for m in mixed], axis=1))
        gm = jnp.concatenate(gm, axis=0).reshape(nb, CHUNK, D_GMLP) + bsp_ref[...][None]
        ygm = flat(ug_ref[bs] * gm)
        gmn = _rms(ygm, ggm_ref[...]).astype(bf16)
        proj = (jnp.dot(flat(s5n_ref[bs]), wmo_ref[:D_S5, :], preferred_element_type=f32)
                + jnp.dot(gmn, wmo_ref[D_S5:, :], preferred_element_type=f32))
        x2 = flat(x1_ref[bs]) + _per_batch(proj, gate_ref[bs], jnp.multiply)
        y = _swiglu_step(x2, mod_ref[0, bs], mod_ref[1, bs], mod_ref[2, bs], g2_ref, win_ref, wout_ref)
        o_ref[bs] = _rms(y, gf_ref[...]).reshape(nb, CHUNK, D_MODEL)


def _tail(x1, s5n, ug, v, mod, p):
    seq = x1.shape[1]
    blk = lambda w: pl.BlockSpec((BATCH, CHUNK, w), lambda i: (0, i, 0))
    return pl.pallas_call(
        _tail_kernel,
        out_shape=jax.ShapeDtypeStruct((BATCH, seq, D_MODEL), f32),
        grid=(seq // CHUNK,),
        in_specs=[blk(D_MODEL), blk(D_S5), blk(D_GMLP), blk(D_GMLP),
                  _const_spec((GMLP_HEADS, CHUNK, CHUNK)),
                  _const_spec((CHUNK, D_GMLP)),
                  _const_spec((1, D_GMLP)),
                  _const_spec((D_MODEL, D_MODEL)),
                  _const_spec((BATCH, D_MODEL)),
                  _const_spec((3, BATCH, D_MODEL)),
                  _const_spec((1, D_MODEL)),
                  _const_spec((D_MODEL, 2 * D_FF)),
                  _const_spec((D_FF, D_MODEL)),
                  _const_spec((1, D_MODEL))],
        out_specs=blk(D_MODEL),
        compiler_params=pltpu.CompilerParams(
            dimension_semantics=("parallel",), vmem_limit_bytes=VMEM_LIMIT),
        name="tail",
    )(x1, s5n, ug, v, p["wsp"], p["bsp"], p["g_gm"], p["w_mix_out"], mod[5],
      mod[6:9], p["g2"], p["ffn2_in"], p["ffn2_out"], p["g_final"])


def _trunk(x, mod, p):
    bsz, seq, _ = x.shape
    assert bsz == BATCH and seq % T_BLK == 0 and CHUNK == T_BLK
    mod = mod.reshape(BATCH, N_MOD, D_MODEL).transpose(1, 0, 2)
    x1, u, ug, v = _ffn1_mixin(x, mod, p)
    yf = _s5_fwd(u, p["s5_f"])
    s5n = _s5_bwd_glu(u, yf, p["s5_b"], p["d"], p["w_glu"], p["g_s5"])
    return _tail(x1, s5n, ug, v, mod, p)


def kernel(x_prompt, x_sample, c_prompt, c_sample, w_ada, b_ada, norm_ffn1_g, ffn1_w_in, ffn1_w_out, norm_mix_g, w_mix_in, s5_lam_re_f, s5_lam_im_f, s5_log_step_f, s5_b_re_f, s5_b_im_f, s5_c_re_f, s5_c_im_f, s5_lam_re_b, s5_lam_im_b, s5_log_step_b, s5_b_re_b, s5_b_im_b, s5_c_re_b, s5_c_im_b, s5_d, s5_w_glu, gmlp_ln_g, gmlp_ln_b, gmlp_w_sp, gmlp_b_sp, norm_out_s5_g, norm_out_gmlp_g, w_mix_out, norm_ffn2_g, ffn2_w_in, ffn2_w_out, final_norm_g):
    assert w_ada.shape[0] == 1, "single layer"
    row = lambda a: a.reshape(1, -1)
    both = lambda f, b: jnp.stack([f[0], b[0]])

    prep = _s5_prep(
        both(s5_lam_re_f, s5_lam_re_b), both(s5_lam_im_f, s5_lam_im_b),
        both(s5_log_step_f, s5_log_step_b), both(s5_b_re_f, s5_b_re_b),
        both(s5_b_im_f, s5_b_im_b), both(s5_c_re_f, s5_c_re_b), both(s5_c_im_f, s5_c_im_b))
    bsp = jnp.repeat(gmlp_b_sp[0].T, GMLP_HEAD_DIM, axis=1)

    p = dict(
        g1=row(norm_ffn1_g), ffn1_in=ffn1_w_in[0].astype(bf16), ffn1_out=ffn1_w_out[0].astype(bf16),
        g_mix=row(norm_mix_g), w_mix_in=w_mix_in[0].astype(bf16),
        ln_g=row(gmlp_ln_g), ln_b=row(gmlp_ln_b),
        s5_f=_s5_weights(prep, s5_c_re_f[0], s5_c_im_f[0], 0, reverse=False),
        s5_b=_s5_weights(prep, s5_c_re_b[0], s5_c_im_b[0], 1, reverse=True),
        d=row(s5_d), w_glu=s5_w_glu[0].astype(bf16), wsp=gmlp_w_sp[0].astype(bf16), bsp=bsp,
        g_s5=row(norm_out_s5_g), g_gm=row(norm_out_gmlp_g), w_mix_out=w_mix_out[0].astype(bf16),
        g2=row(norm_ffn2_g), ffn2_in=ffn2_w_in[0].astype(bf16), ffn2_out=ffn2_w_out[0].astype(bf16),
        g_final=row(final_norm_g),
    )

    c = jnp.concatenate([c_prompt, c_sample], axis=0)
    mod = _ada(c, w_ada[0], b_ada)
    nb = c_prompt.shape[0]
    return (_trunk(x_prompt, mod[:nb], p), _trunk(x_sample, mod[nb:], p))
```

```python
import jax
import jax.numpy as jnp
from jax import lax
from jax.experimental import pallas as pl
from jax.experimental.pallas import tpu as pltpu

D_MODEL = 1024
BATCH = 8
D_S5 = 512
S5_GROUPS = 32
S5_GROUP = 16
S5_STATE = 64
N_STATE = S5_GROUPS * S5_STATE
D_GMLP = 512
GMLP_HEADS = 4
GMLP_HEAD_DIM = 128
CHUNK = 128
D_FF = 2816
N_MOD = 9
EPS = 1e-6

LANES = 128
PAIR_COLS = 2 * LANES
N_PAIRS = S5_GROUPS // 2
VMEM_LIMIT = 56 * 1024 * 1024

T_FFN = 128
TM_FFN = T_FFN * BATCH
FFN_SUBTILES = 4
TAIL_SUBTILES = 4
T_BLK = 128
ROWS_BLK = T_BLK * BATCH
HALF_ROWS = ROWS_BLK // 2
SCAN_COLS = 512

f32 = jnp.float32
bf16 = jnp.bfloat16


def _const_spec(shape):
    nd = len(shape)
    return pl.BlockSpec(shape, lambda i: (0,) * nd, pipeline_mode=pl.Buffered(1))


def _rms(x, g):
    ms = jnp.mean(x * x, axis=-1, keepdims=True)
    return x * lax.rsqrt(ms + EPS) * g


def _per_batch(x, v, op):
    rows, d = x.shape
    nb = v.shape[0]
    return op(x.reshape(nb, rows // nb, d), v[:, None, :]).reshape(rows, d)


def _modulate(h, shift, scale):
    return _per_batch(_per_batch(h, 1.0 + scale, jnp.multiply), shift, jnp.add)


def _to_time_major(x, nb):
    rows, d = x.shape
    return jnp.swapaxes(x.reshape(nb, rows // nb, d), 0, 1).reshape(rows, d)


def _to_batch_major(x, nb):
    rows, d = x.shape
    return jnp.swapaxes(x.reshape(rows // nb, nb, d), 0, 1)


def _ada_kernel(c_ref, w_ref, b_ref, o_ref):
    c = c_ref[...]
    s = c * jax.nn.sigmoid(c)
    o_ref[...] = jnp.dot(s.astype(bf16), w_ref[...].astype(bf16),
                         preferred_element_type=f32) + b_ref[...]


def _ada(c, w, b):
    n = c.shape[0]
    return pl.pallas_call(
        _ada_kernel,
        out_shape=jax.ShapeDtypeStruct((n, N_MOD * D_MODEL), f32),
        grid=(N_MOD,),
        in_specs=[pl.BlockSpec((n, D_MODEL), lambda j: (0, 0)),
                  pl.BlockSpec((D_MODEL, D_MODEL), lambda j: (0, j)),
                  pl.BlockSpec((1, D_MODEL), lambda j: (0, j))],
        out_specs=pl.BlockSpec((n, D_MODEL), lambda j: (0, j)),
        compiler_params=pltpu.CompilerParams(dimension_semantics=("arbitrary",)),
        name="ada",
    )(c, w, b)


def _swiglu_step(x, shift, scale, gate, g_ref, win_ref, wout_ref):
    h = _modulate(_rms(x, g_ref[...]), shift, scale).astype(bf16)
    a = jnp.dot(h, win_ref[...], preferred_element_type=f32)
    act = (jax.nn.silu(a[:, :D_FF]) * a[:, D_FF:]).astype(bf16)
    f = jnp.dot(act, wout_ref[...], preferred_element_type=f32)
    return x + _per_batch(f, 0.5 * gate, jnp.multiply)


def _ffn1_mixin_kernel(x_ref, mod1_ref, g1_ref, win_ref, wout_ref,
                       mod2_ref, g2_ref, wmix_ref, lng_ref, lnb_ref,
                       x1_ref, u_ref, ug_ref, v_ref):
    sub_t = T_FFN // FFN_SUBTILES
    sub_rows = sub_t * BATCH
    for s in range(FFN_SUBTILES):
        ts = slice(s * sub_t, (s + 1) * sub_t)
        slab = lambda a: a.reshape(BATCH, sub_t, a.shape[-1])
        x = x_ref[:, ts, :].reshape(sub_rows, D_MODEL)
        x1 = _swiglu_step(x, mod1_ref[0], mod1_ref[1], mod1_ref[2], g1_ref, win_ref, wout_ref)
        x1_ref[:, ts, :] = slab(x1)
        h = _modulate(_rms(x1, g2_ref[...]), mod2_ref[0], mod2_ref[1]).astype(bf16)
        z = jnp.dot(h, wmix_ref[...], preferred_element_type=f32)
        u_ref[s * sub_rows:(s + 1) * sub_rows, :] = _to_time_major(z[:, :D_S5], BATCH)
        zg = jax.nn.gelu(z[:, D_S5:])
        ug_ref[:, ts, :] = slab(zg[:, :D_GMLP])
        v = zg[:, D_GMLP:]
        mu = jnp.mean(v, axis=-1, keepdims=True)
        vc = v - mu
        var = jnp.mean(vc * vc, axis=-1, keepdims=True)
        v_ref[:, ts, :] = slab((vc * lax.rsqrt(var + EPS) * lng_ref[...] + lnb_ref[...]).astype(bf16))


def _ffn1_mixin(x, mod, p):
    seq = x.shape[1]
    rows = seq * BATCH
    slab_blk = lambda w: pl.BlockSpec((BATCH, T_FFN, w), lambda i: (0, i, 0))
    return pl.pallas_call(
        _ffn1_mixin_kernel,
        out_shape=(jax.ShapeDtypeStruct((BATCH, seq, D_MODEL), f32),
                   jax.ShapeDtypeStruct((rows, D_S5), f32),
                   jax.ShapeDtypeStruct((BATCH, seq, D_GMLP), f32),
                   jax.ShapeDtypeStruct((BATCH, seq, D_GMLP), bf16)),
        grid=(rows // TM_FFN,),
        in_specs=[slab_blk(D_MODEL),
                  _const_spec((3, BATCH, D_MODEL)),
                  _const_spec((1, D_MODEL)),
                  _const_spec((D_MODEL, 2 * D_FF)),
                  _const_spec((D_FF, D_MODEL)),
                  _const_spec((3, BATCH, D_MODEL)),
                  _const_spec((1, D_MODEL)),
                  _const_spec((D_MODEL, D_S5 + 2 * D_GMLP)),
                  _const_spec((1, D_GMLP)),
                  _const_spec((1, D_GMLP))],
        out_specs=(slab_blk(D_MODEL), pl.BlockSpec((TM_FFN, D_S5), lambda i: (i, 0)),
                   slab_blk(D_GMLP), slab_blk(D_GMLP)),
        compiler_params=pltpu.CompilerParams(
            dimension_semantics=("parallel",), vmem_limit_bytes=VMEM_LIMIT),
        name="ffn1_mixin",
    )(x, mod[0:3], p["g1"], p["ffn1_in"], p["ffn1_out"],
      mod[3:6], p["g_mix"], p["w_mix_in"], p["ln_g"], p["ln_b"])


def _s5_prep_kernel(lre_ref, lim_ref, ls_ref, bre_ref, bim_ref, cre_ref, cim_ref,
                    a2f_ref, a2b_ref, wbf_ref, wbb_ref, wctf_ref, wctb_ref, wcbf_ref, wcbb_ref):
    lre = lre_ref[...]
    lim = lim_ref[...]
    dt = jnp.exp(ls_ref[...])
    mag = jnp.exp(lre * dt)
    ab_re = mag * jnp.cos(lim * dt)
    ab_im = mag * jnp.sin(lim * dt)
    n_re = ab_re - 1.0
    n_im = ab_im
    den = lre * lre + lim * lim
    f_re = (n_re * lre + n_im * lim) / den
    f_im = (n_im * lre - n_re * lim) / den
    a2_re = ab_re * ab_re - ab_im * ab_im
    a2_im = 2.0 * ab_re * ab_im
    lane = lax.broadcasted_iota(jnp.int32, (1, LANES), 1)
    half_groups = S5_GROUPS // 2
    half_states = N_STATE // 2

    for d, (a2_ref, wb_ref, wct_ref, wcb_ref) in enumerate(
            ((a2f_ref, wbf_ref, wctf_ref, wcbf_ref), (a2b_ref, wbb_ref, wctb_ref, wcbb_ref))):
        reverse = d == 1
        a2_ref[0:1, :] = a2_re[d:d + 1]
        a2_ref[1:2, :] = a2_im[d:d + 1]
        wb_ref[...] = jnp.zeros_like(wb_ref)
        wct_ref[...] = jnp.zeros_like(wct_ref)
        wcb_ref[...] = jnp.zeros_like(wcb_ref)

        fr, fi = f_re[d:d + 1], f_im[d:d + 1]
        ar, ai = ab_re[d:d + 1], ab_im[d:d + 1]
        bre, bim, cre, cim = bre_ref[d], bim_ref[d], cre_ref[d], cim_ref[d]
        bb_re = fr * bre - fi * bim
        bb_im = fr * bim + fi * bre
        plain = (bb_re, bb_im)
        stepped = (ar * bb_re - ai * bb_im, ar * bb_im + ai * bb_re)
        ca = (cre * ar - cim * ai, cre * ai + cim * ar)
        by_step = (plain, stepped) if reverse else (stepped, plain)

        for g in range(S5_GROUPS):
            gl = slice(g * S5_STATE, (g + 1) * S5_STATE)
            i, j = divmod(g, 2)
            cols_re = slice(j * S5_STATE, (j + 1) * S5_STATE)
            cols_im = slice(LANES + j * S5_STATE, LANES + (j + 1) * S5_STATE)
            for t, (re, im) in enumerate(by_step):
                r0 = t * LANES + (g * S5_GROUP) % LANES
                wb_ref[i, r0:r0 + S5_GROUP, cols_re] = re[:, gl].astype(bf16)
                wb_ref[i, r0:r0 + S5_GROUP, cols_im] = im[:, gl].astype(bf16)
            hf, gk = divmod(g, half_groups)
            k0 = gk * S5_STATE
            for v, (re, im) in enumerate((ca, (cre, cim))):
                n0 = v * (D_S5 // 2) + gk * S5_GROUP
                wct_ref[hf, n0:n0 + S5_GROUP, k0:k0 + S5_STATE] = re[:, gl].astype(bf16)
                wct_ref[hf, n0:n0 + S5_GROUP, half_states + k0:half_states + k0 + S5_STATE] = (
                    -im[:, gl]).astype(bf16)

        for i in range(N_PAIRS):
            tl = slice(i * LANES, (i + 1) * LANES)
            rhs = jnp.concatenate([cre[:, tl], -cim[:, tl]], axis=1)
            for j in range(2):
                g = 2 * i + j
                own = (lane // S5_STATE) == j
                lhs = jnp.concatenate([jnp.where(own, bb_re[:, tl], 0.0),
                                       jnp.where(own, bb_im[:, tl], 0.0)], axis=1)
                blk = lax.dot_general(lhs, rhs, (((1,), (1,)), ((), ())),
                                      precision=lax.Precision.HIGHEST, preferred_element_type=f32)
                wcb_ref[g * S5_GROUP:(g + 1) * S5_GROUP, g * S5_GROUP:(g + 1) * S5_GROUP] = blk.astype(bf16)


def _s5_prep(lam_re, lam_im, log_step, b_re, b_im, c_re, c_im):
    flat = lambda a: a.reshape(2, N_STATE)
    ls = jnp.broadcast_to(log_step[:, :, None], (2, S5_GROUPS, S5_STATE))
    bt = lambda b: b.transpose(0, 3, 1, 2).reshape(2, S5_GROUP, N_STATE)
    ct = lambda c: c.transpose(0, 2, 1, 3).reshape(2, S5_GROUP, N_STATE)
    per_dir = lambda shape, dt: (jax.ShapeDtypeStruct(shape, dt),) * 2
    a2f, a2b, wbf, wbb, wctf, wctb, wcbf, wcbb = pl.pallas_call(
        _s5_prep_kernel,
        out_shape=(per_dir((2, N_STATE), f32) + per_dir((N_PAIRS, PAIR_COLS, PAIR_COLS), bf16)
                   + per_dir((2, D_S5, N_STATE), bf16) + per_dir((D_S5, D_S5), bf16)),
        compiler_params=pltpu.CompilerParams(vmem_limit_bytes=VMEM_LIMIT),
        name="s5_prep",
    )(flat(lam_re), flat(lam_im), flat(ls), bt(b_re), bt(b_im), ct(c_re), ct(c_im))
    return (dict(a2=a2f, wb=wbf, wct=wctf, wcb=wcbf), dict(a2=a2b, wb=wbb, wct=wctb, wcb=wcbb))


def _s5_outputs(u_ref, wb_ref, a2_ref, wct_ref, wcb_ref, w_re_ref, w_im_ref, xs_re_ref, xs_im_ref,
                st_ref, reverse):
    n_steps = T_BLK // 2
    tile = lambda j: slice(j * BATCH, (j + 1) * BATCH)

    @pl.when(pl.program_id(0) == 0)
    def _():
        st_ref[...] = jnp.zeros_like(st_ref)

    u4 = u_ref[...].reshape(n_steps, 2, BATCH, D_S5)
    u_even = u4[:, 0].reshape(HALF_ROWS, D_S5).astype(bf16)
    u_odd = u4[:, 1].reshape(HALF_ROWS, D_S5).astype(bf16)

    for i in range(N_PAIRS):
        lane0 = (i * 2 * S5_GROUP // LANES) * LANES
        lhs = jnp.concatenate([u_even[:, lane0:lane0 + LANES], u_odd[:, lane0:lane0 + LANES]], axis=1)
        r = jnp.dot(lhs, wb_ref[i], preferred_element_type=f32)
        w_re_ref[:, i * LANES:(i + 1) * LANES] = r[:, :LANES]
        w_im_ref[:, i * LANES:(i + 1) * LANES] = r[:, LANES:]

    zero_tile = jnp.zeros((BATCH, SCAN_COLS), f32)
    for cb in range(N_STATE // SCAN_COLS):
        cols = slice(cb * SCAN_COLS, (cb + 1) * SCAN_COLS)
        ar = jnp.broadcast_to(a2_ref[0:1, cols], (BATCH, SCAN_COLS))
        ai = jnp.broadcast_to(a2_ref[1:2, cols], (BATCH, SCAN_COLS))

        def put_pair(j, lo, hi):
            rows = slice(j * BATCH, (j + 2) * BATCH)
            xs_re_ref[rows, cols] = jnp.concatenate([lo[0], hi[0]], axis=0).astype(bf16)
            xs_im_ref[rows, cols] = jnp.concatenate([lo[1], hi[1]], axis=0).astype(bf16)

        state = (st_ref[0, :, cols], st_ref[1, :, cols])
        pad = (zero_tile, zero_tile)
        held = None
        if reverse:
            put_pair(n_steps, state, pad)
        else:
            held = state
        for k in range(n_steps):
            kk = (n_steps - 1 - k) if reverse else k
            sr, si = state
            state = (ar * sr - ai * si + w_re_ref[tile(kk), cols],
                     ar * si + ai * sr + w_im_ref[tile(kk), cols])
            j = kk if reverse else kk + 1
            if held is None:
                held = state
            elif reverse:
                put_pair(j, state, held)
                held = None
            else:
                put_pair(j - 1, held, state)
                held = None
        if not reverse:
            put_pair(n_steps, held, pad)
        st_ref[0, :, cols] = state[0]
        st_ref[1, :, cols] = state[1]

    half = N_STATE // 2
    width = D_S5 // 2
    u_skip = u_odd if reverse else u_even
    own0, skip0 = (0, BATCH) if reverse else (BATCH, 0)
    y_own, y_skip = [], []
    for hf in range(2):
        hc = slice(hf * half, (hf + 1) * half)
        lhs = jnp.concatenate([xs_re_ref[:, hc], xs_im_ref[:, hc]], axis=1)
        r = lax.dot_general(lhs, wct_ref[hf], (((1,), (1,)), ((), ())), preferred_element_type=f32)
        y_own.append(r[own0:own0 + HALF_ROWS, width:])
        y_skip.append(r[skip0:skip0 + HALF_ROWS, :width]
                      + jnp.dot(u_skip, wcb_ref[:, hf * width:(hf + 1) * width],
                                preferred_element_type=f32))
    y_own = jnp.concatenate(y_own, axis=1).reshape(n_steps, 1, BATCH, D_S5)
    y_skip = jnp.concatenate(y_skip, axis=1).reshape(n_steps, 1, BATCH, D_S5)
    pair = (y_own, y_skip) if reverse else (y_skip, y_own)
    return jnp.concatenate(pair, axis=1).reshape(ROWS_BLK, D_S5)


def _s5_fwd_kernel(u_ref, wb_ref, a2_ref, wct_ref, wcb_ref, y_ref, *scratch):
    y_ref[...] = _s5_outputs(u_ref, wb_ref, a2_ref, wct_ref, wcb_ref, *scratch, reverse=False)


def _s5_bwd_kernel(u_ref, wb_ref, a2_ref, wct_ref, wcb_ref, yf_ref, d_ref, wglu_ref,
                   g_ref, o_ref, *scratch):
    yb = _s5_outputs(u_ref, wb_ref, a2_ref, wct_ref, wcb_ref, *scratch, reverse=True)
    y = jax.nn.gelu(yf_ref[...] + yb + d_ref[...] * u_ref[...])
    y = y * jax.nn.sigmoid(jnp.dot(y.astype(bf16), wglu_ref[...], preferred_element_type=f32))
    o_ref[...] = _to_batch_major(_rms(y, g_ref[...]), BATCH).astype(bf16)


def _s5_specs(idx):
    return [pl.BlockSpec((ROWS_BLK, D_S5), idx),
            _const_spec((N_PAIRS, PAIR_COLS, PAIR_COLS)),
            _const_spec((2, N_STATE)),
            _const_spec((2, D_S5, N_STATE)),
            _const_spec((D_S5, D_S5))]


STATE_ROWS = HALF_ROWS + 2 * BATCH
_S5_SCRATCH = [pltpu.VMEM((HALF_ROWS, N_STATE), f32),
               pltpu.VMEM((HALF_ROWS, N_STATE), f32),
               pltpu.VMEM((STATE_ROWS, N_STATE), bf16),
               pltpu.VMEM((STATE_ROWS, N_STATE), bf16),
               pltpu.VMEM((2, BATCH, N_STATE), f32)]
_S5_PARAMS = pltpu.CompilerParams(dimension_semantics=("arbitrary",), vmem_limit_bytes=VMEM_LIMIT)


def _s5_fwd(u, w):
    rows = u.shape[0]
    idx = lambda i: (i, 0)
    return pl.pallas_call(
        _s5_fwd_kernel,
        out_shape=jax.ShapeDtypeStruct((rows, D_S5), f32),
        grid=(rows // ROWS_BLK,),
        in_specs=_s5_specs(idx),
        out_specs=pl.BlockSpec((ROWS_BLK, D_S5), idx),
        scratch_shapes=_S5_SCRATCH,
        compiler_params=_S5_PARAMS,
        name="s5_fwd",
    )(u, w["wb"], w["a2"], w["wct"], w["wcb"])


def _s5_bwd_glu(u, yf, w, d, w_glu, g_s5):
    rows = u.shape[0]
    nblk = rows // ROWS_BLK
    idx = lambda i: (nblk - 1 - i, 0)
    return pl.pallas_call(
        _s5_bwd_kernel,
        out_shape=jax.ShapeDtypeStruct((BATCH, rows // BATCH, D_S5), bf16),
        grid=(nblk,),
        in_specs=_s5_specs(idx) + [pl.BlockSpec((ROWS_BLK, D_S5), idx),
                                   _const_spec((1, D_S5)),
                                   _const_spec((D_S5, D_S5)),
                                   _const_spec((1, D_S5))],
        out_specs=pl.BlockSpec((BATCH, T_BLK, D_S5), lambda i: (0, nblk - 1 - i, 0)),
        scratch_shapes=_S5_SCRATCH,
        compiler_params=_S5_PARAMS,
        name="s5_bwd_glu",
    )(u, w["wb"], w["a2"], w["wct"], w["wcb"], yf, d, w_glu, g_s5)


def _tail_kernel(x1_ref, s5n_ref, ug_ref, v_ref, wsp_ref, bsp_ref, ggm_ref, wmo_ref, gate_ref,
                 mod_ref, g2_ref, win_ref, wout_ref, gf_ref, o_ref):
    mixed = []
    for h in range(GMLP_HEADS):
        hc = slice(h * GMLP_HEAD_DIM, (h + 1) * GMLP_HEAD_DIM)
        rhs = jnp.concatenate([v_ref[b, :, hc] for b in range(BATCH)], axis=1)
        mixed.append(jnp.dot(wsp_ref[h], rhs, preferred_element_type=f32))

    nb = BATCH // TAIL_SUBTILES
    for s in range(TAIL_SUBTILES):
        bs = slice(s * nb, (s + 1) * nb)
        flat = lambda a: a.reshape(nb * CHUNK, a.shape[-1])
        gm = []
        for b in range(s * nb, (s + 1) * nb):
            lanes = slice(b * GMLP_HEAD_DIM, (b + 1) * GMLP_HEAD_DIM)
            gm.append(jnp.concatenate([m[:, lanes] for m in mixed], axis=1))
        gm = jnp.concatenate(gm, axis=0).reshape(nb, CHUNK, D_GMLP) + bsp_ref[...][None]
        ygm = flat(ug_ref[bs] * gm)
        gmn = _rms(ygm, ggm_ref[...]).astype(bf16)
        proj = (jnp.dot(flat(s5n_ref[bs]), wmo_ref[:D_S5, :], preferred_element_type=f32)
                + jnp.dot(gmn, wmo_ref[D_S5:, :], preferred_element_type=f32))
        x2 = flat(x1_ref[bs]) + _per_batch(proj, gate_ref[bs], jnp.multiply)
        y = _swiglu_step(x2, mod_ref[0, bs], mod_ref[1, bs], mod_ref[2, bs], g2_ref, win_ref, wout_ref)
        o_ref[bs] = _rms(y, gf_ref[...]).reshape(nb, CHUNK, D_MODEL)


def _tail(x1, s5n, ug, v, mod, p):
    seq = x1.shape[1]
    blk = lambda w: pl.BlockSpec((BATCH, CHUNK, w), lambda i: (0, i, 0))
    return pl.pallas_call(
        _tail_kernel,
        out_shape=jax.ShapeDtypeStruct((BATCH, seq, D_MODEL), f32),
        grid=(seq // CHUNK,),
        in_specs=[blk(D_MODEL), blk(D_S5), blk(D_GMLP), blk(D_GMLP),
                  _const_spec((GMLP_HEADS, CHUNK, CHUNK)),
                  _const_spec((CHUNK, D_GMLP)),
                  _const_spec((1, D_GMLP)),
                  _const_spec((D_MODEL, D_MODEL)),
                  _const_spec((BATCH, D_MODEL)),
                  _const_spec((3, BATCH, D_MODEL)),
                  _const_spec((1, D_MODEL)),
                  _const_spec((D_MODEL, 2 * D_FF)),
                  _const_spec((D_FF, D_MODEL)),
                  _const_spec((1, D_MODEL))],
        out_specs=blk(D_MODEL),
        compiler_params=pltpu.CompilerParams(
            dimension_semantics=("parallel",), vmem_limit_bytes=VMEM_LIMIT),
        name="tail",
    )(x1, s5n, ug, v, p["wsp"], p["bsp"], p["g_gm"], p["w_mix_out"], mod[5],
      mod[6:9], p["g2"], p["ffn2_in"], p["ffn2_out"], p["g_final"])


def _trunk(x, mod, p):
    bsz, seq, _ = x.shape
    assert bsz == BATCH and seq % T_BLK == 0 and CHUNK == T_BLK
    mod = mod.reshape(BATCH, N_MOD, D_MODEL).transpose(1, 0, 2)
    x1, u, ug, v = _ffn1_mixin(x, mod, p)
    yf = _s5_fwd(u, p["s5_f"])
    s5n = _s5_bwd_glu(u, yf, p["s5_b"], p["d"], p["w_glu"], p["g_s5"])
    return _tail(x1, s5n, ug, v, mod, p)


def kernel(x_prompt, x_sample, c_prompt, c_sample, w_ada, b_ada, norm_ffn1_g, ffn1_w_in, ffn1_w_out, norm_mix_g, w_mix_in, s5_lam_re_f, s5_lam_im_f, s5_log_step_f, s5_b_re_f, s5_b_im_f, s5_c_re_f, s5_c_im_f, s5_lam_re_b, s5_lam_im_b, s5_log_step_b, s5_b_re_b, s5_b_im_b, s5_c_re_b, s5_c_im_b, s5_d, s5_w_glu, gmlp_ln_g, gmlp_ln_b, gmlp_w_sp, gmlp_b_sp, norm_out_s5_g, norm_out_gmlp_g, w_mix_out, norm_ffn2_g, ffn2_w_in, ffn2_w_out, final_norm_g):
    assert w_ada.shape[0] == 1, "single layer"
    row = lambda a: a.reshape(1, -1)
    both = lambda f, b: jnp.stack([f[0], b[0]])

    s5_f, s5_b = _s5_prep(
        both(s5_lam_re_f, s5_lam_re_b), both(s5_lam_im_f, s5_lam_im_b),
        both(s5_log_step_f, s5_log_step_b), both(s5_b_re_f, s5_b_re_b),
        both(s5_b_im_f, s5_b_im_b), both(s5_c_re_f, s5_c_re_b), both(s5_c_im_f, s5_c_im_b))
    bsp = jnp.repeat(gmlp_b_sp[0].T, GMLP_HEAD_DIM, axis=1)

    p = dict(
        g1=row(norm_ffn1_g), ffn1_in=ffn1_w_in[0].astype(bf16), ffn1_out=ffn1_w_out[0].astype(bf16),
        g_mix=row(norm_mix_g), w_mix_in=w_mix_in[0].astype(bf16),
        ln_g=row(gmlp_ln_g), ln_b=row(gmlp_ln_b),
        s5_f=s5_f, s5_b=s5_b,
        d=row(s5_d), w_glu=s5_w_glu[0].astype(bf16), wsp=gmlp_w_sp[0].astype(bf16), bsp=bsp,
        g_s5=row(norm_out_s5_g), g_gm=row(norm_out_gmlp_g), w_mix_out=w_mix_out[0].astype(bf16),
        g2=row(norm_ffn2_g), ffn2_in=ffn2_w_in[0].astype(bf16), ffn2_out=ffn2_w_out[0].astype(bf16),
        g_final=row(final_norm_g),
    )

    c = jnp.concatenate([c_prompt, c_sample], axis=0)
    mod = _ada(c, w_ada[0], b_ada)
    nb = c_prompt.shape[0]
    return (_trunk(x_prompt, mod[:nb], p), _trunk(x_sample, mod[nb:], p))
```
